```python
import jax, jax.numpy as jnp
from jax import lax
import numpy as np

D_MODEL = 1024
BATCH = 2
SEQ = 8192
DEPTH = 4

HD = 64
QBLK = 128
A_GROUPS = 3
A_HEADS = 8
A_PATTERNS = ((128, 1), (512, 4), (2048, 16))
A_QKV_W = A_GROUPS * A_HEADS * HD
BRANCH_W = A_HEADS * HD
B_HEADS = 8
B_KV = 2
NSA_CMP_BLK = 32
NSA_CMP_STRIDE = 16
NSA_CMP_HIDDEN = 256
NSA_SEL_BLK = 64
NSA_N_SEL = 16
NSA_WINDOW = 512
C_HEADS = 8
C_KV = 2
C_WINDOW = 128
N_BRANCH = 3
DEEPNORM_ALPHA = (2 * DEPTH) ** 0.25
DEEPNORM_BETA = (8 * DEPTH) ** -0.25
LN_EPS = 1e-5
NEG = -1e30
FORCE_SCORE = 1e4
ATTN_SCALE = HD ** -0.5
IN_WIDTHS = (A_QKV_W, A_QKV_W, A_QKV_W, BRANCH_W,
             B_HEADS * HD, B_KV * HD, B_KV * HD, B_KV * HD, B_KV * HD, B_KV * HD, B_KV * HD,
             BRANCH_W, B_HEADS * 3,
             C_HEADS * HD, C_KV * HD, C_KV * HD, BRANCH_W,
             N_BRANCH * D_MODEL)
IN_COLS = sum(IN_WIDTHS)

kernel_name = "hybrid_dilated_nsa_swa_gated_deepnorm"


def alibi_slopes(n):
    return 2.0 ** (-8.0 * jnp.arange(1, n + 1, dtype=jnp.float32) / n)


def layer_norm(x, g, b):
    xf = x.astype(jnp.float32)
    mu = jnp.mean(xf, axis=-1, keepdims=True)
    var = jnp.mean(jnp.square(xf - mu), axis=-1, keepdims=True)
    return ((xf - mu) * lax.rsqrt(var + LN_EPS) * g.astype(jnp.float32) + b.astype(jnp.float32)).astype(x.dtype)


def banded_attention(q, k, v, slopes, window, dist_scale, sinks=None):
    B, L, H, Dh = q.shape
    Hkv = k.shape[2]
    G = H // Hkv
    nblk = -(-L // QBLK)
    Lp = nblk * QBLK
    nb = -(-window // QBLK)
    pad = Lp - L
    qp = jnp.pad(q, ((0, 0), (0, pad), (0, 0), (0, 0))).reshape(B, nblk, QBLK, Hkv, G, Dh)
    kp = jnp.pad(k, ((0, 0), (nb * QBLK, pad), (0, 0), (0, 0))).reshape(B, nblk + nb, QBLK, Hkv, Dh)
    vp = jnp.pad(v, ((0, 0), (nb * QBLK, pad), (0, 0), (0, 0))).reshape(B, nblk + nb, QBLK, Hkv, Dh)
    kc = jnp.concatenate([kp[:, i:i + nblk] for i in range(nb + 1)], axis=2)
    vc = jnp.concatenate([vp[:, i:i + nblk] for i in range(nb + 1)], axis=2)
    C = (nb + 1) * QBLK
    qi = jnp.arange(QBLK)[:, None]
    ci = jnp.arange(C)[None, :]
    dist = qi - ci + nb * QBLK
    kpos = jnp.arange(nblk)[:, None, None] * QBLK + ci[None] - nb * QBLK
    valid = (dist >= 0)[None] & (dist <= window)[None] & (kpos >= 0)
    s = jnp.einsum('bnikgd,bnckd->bnkgic', qp.astype(jnp.float32), kc.astype(jnp.float32)) * ATTN_SCALE
    s = s - (slopes.reshape(Hkv, G)[:, :, None, None] * dist_scale) * dist.astype(jnp.float32)
    s = jnp.where(valid[None, :, None, None], s, NEG)
    lse = jax.nn.logsumexp(s, axis=-1)
    if sinks is not None:
        lse = jnp.logaddexp(lse, sinks.astype(jnp.float32).reshape(Hkv, G)[:, :, None])
    p = jnp.exp(s - lse[..., None])
    o = jnp.einsum('bnkgic,bnckd->bnikgd', p, vc.astype(jnp.float32))
    o = o.reshape(B, Lp, H, Dh)[:, :L].astype(q.dtype)
    lse = lse.transpose(0, 1, 4, 2, 3).reshape(B, Lp, H)[:, :L]
    return o, lse


def dilated_attention(q, k, v, slopes, window, dilation):
    B, S, H, Dh = q.shape
    r = dilation

    def fold(t):
        return t.reshape(B, S // r, r, t.shape[2], Dh).transpose(0, 2, 1, 3, 4).reshape(B * r, S // r, t.shape[2], Dh)

    o, lse = banded_attention(fold(q), fold(k), fold(v), slopes, window // r, r)
    o = o.reshape(B, r, S // r, H, Dh).transpose(0, 2, 1, 3, 4).reshape(B, S, H, Dh)
    lse = lse.reshape(B, r, S // r, H).transpose(0, 2, 1, 3).reshape(B, S, H)
    return o, lse


def nsa_compress(x, w1, w2, pos):
    B, S, Hkv, Dh = x.shape
    ch = x.reshape(B, S // NSA_CMP_STRIDE, NSA_CMP_STRIDE, Hkv, Dh)
    blocks = jnp.concatenate([ch[:, :-1], ch[:, 1:]], axis=2)
    blocks = blocks + pos[None, None, :, None, :]
    ncmp = blocks.shape[1]
    flat = blocks.transpose(0, 1, 3, 2, 4).reshape(B, ncmp, Hkv, NSA_CMP_BLK * Dh)
    return jax.nn.gelu(flat @ w1) @ w2


def nsa_compressed_and_selected(q, kc, vc, ks, vs, slopes):
    B, S, H, Dh = q.shape
    Hkv = ks.shape[2]
    G = H // Hkv
    ncmp = kc.shape[1]
    nsel = S // NSA_SEL_BLK
    n_pick = min(NSA_N_SEL, nsel)
    nblk = S // QBLK
    cmp_start = jnp.arange(ncmp) * NSA_CMP_STRIDE
    cmp_end = cmp_start + NSA_CMP_BLK - 1
    sel_start = jnp.arange(nsel) * NSA_SEL_BLK
    overlap = ((cmp_start[:, None] < sel_start[None, :] + NSA_SEL_BLK)
               & (cmp_start[:, None] + NSA_CMP_BLK > sel_start[None, :])).astype(jnp.float32)
    slopes_kg = slopes.reshape(Hkv, G)
    kcf = kc.astype(jnp.float32)
    vcf = vc.astype(jnp.float32)
    kT = ks.transpose(0, 2, 1, 3)
    vT = vs.transpose(0, 2, 1, 3)
    bidx = jnp.arange(B)[:, None, None, None]
    hidx = jnp.arange(Hkv)[None, :, None, None]
    jsel = jnp.arange(nsel)[None, :]
    qb = q.reshape(B, nblk, QBLK, Hkv, G, Dh).transpose(1, 0, 2, 3, 4, 5)

    def block(args):
        n, qn = args
        t = n * QBLK + jnp.arange(QBLK)
        qf = qn.astype(jnp.float32)
        s = jnp.einsum('bikgd,bjkd->bkgij', qf, kcf) * ATTN_SCALE
        dist = (t[:, None] - cmp_end[None, :]).astype(jnp.float32)
        valid = dist >= 0
        s = jnp.where(valid, s - slopes_kg[:, :, None, None] * dist, NEG)
        m = jnp.max(s, axis=-1, keepdims=True)
        e = jnp.where(valid, jnp.exp(s - m), 0.0)
        den = jnp.sum(e, axis=-1, keepdims=True)
        p = e / jnp.where(den > 0, den, 1.0)
        o_cmp = jnp.einsum('bkgij,bjkd->bikgd', p, vcf)
        imp = jnp.einsum('bkgij,js->bkis', p, overlap)
        cur = (t // NSA_SEL_BLK)[:, None]
        forced = (jsel == 0) | (jsel == cur) | (jsel == cur - 1)
        allowed = jsel <= cur
        score = jnp.where(forced, FORCE_SCORE, jnp.where(allowed, imp, -1.0))
        _, idx = lax.top_k(score, n_pick)
        pos = (idx[..., None] * NSA_SEL_BLK + jnp.arange(NSA_SEL_BLK)).reshape(B, Hkv, QBLK, n_pick * NSA_SEL_BLK)
        ksel = kT[bidx, hidx, pos].astype(jnp.float32)
        vsel = vT[bidx, hidx, pos].astype(jnp.float32)
        s2 = jnp.einsum('bikgd,bkitd->bkgit', qf, ksel) * ATTN_SCALE
        d2 = (t[None, None, :, None] - pos)[:, :, None]
        s2 = s2 - slopes_kg[None, :, :, None, None] * d2.astype(jnp.float32)
        s2 = jnp.where(d2 >= 0, s2, NEG)
        p2 = jax.nn.softmax(s2, axis=-1)
        o_sel = jnp.einsum('bkgit,bkitd->bikgd', p2, vsel)
        return o_cmp, o_sel

    o_cmp, o_sel = lax.map(block, (jnp.arange(nblk), qb))
    o_cmp = o_cmp.transpose(1, 0, 2, 3, 4, 5).reshape(B, S, H, Dh).astype(q.dtype)
    o_sel = o_sel.transpose(1, 0, 2, 3, 4, 5).reshape(B, S, H, Dh).astype(q.dtype)
    return o_cmp, o_sel


def setup_inputs(seed: int = 0) -> dict:
    key = jax.random.key(seed)
    ks = jax.random.split(key, 11)
    x = jax.random.normal(ks[0], (BATCH, SEQ, D_MODEL), jnp.float32)
    w_in = jax.random.normal(ks[1], (DEPTH, D_MODEL, IN_COLS), jnp.float32) * D_MODEL ** -0.5
    b_in = 0.02 * jax.random.normal(ks[2], (DEPTH, IN_COLS), jnp.float32)
    w_cmp1 = jax.random.normal(ks[3], (DEPTH, 2, NSA_CMP_BLK * HD, NSA_CMP_HIDDEN), jnp.float32) * (NSA_CMP_BLK * HD) ** -0.5
    w_cmp2 = jax.random.normal(ks[4], (DEPTH, 2, NSA_CMP_HIDDEN, HD), jnp.float32) * NSA_CMP_HIDDEN ** -0.5
    cmp_pos = 0.02 * jax.random.normal(ks[5], (DEPTH, 2, NSA_CMP_BLK, HD), jnp.float32)
    sinks = 0.5 * jax.random.normal(ks[6], (DEPTH, C_HEADS), jnp.float32)
    w_branch = jax.random.normal(ks[7], (DEPTH, N_BRANCH, BRANCH_W, D_MODEL), jnp.float32) * (BRANCH_W ** -0.5 * DEEPNORM_BETA)
    w_out = jax.random.normal(ks[8], (DEPTH, D_MODEL, D_MODEL), jnp.float32) * (D_MODEL ** -0.5 * DEEPNORM_BETA)
    ln_g = 1.0 + 0.02 * jax.random.normal(ks[9], (DEPTH, D_MODEL), jnp.float32)
    ln_b = 0.02 * jax.random.normal(ks[10], (DEPTH, D_MODEL), jnp.float32)
    return {"x": x, "w_in": w_in, "b_in": b_in, "w_cmp1": w_cmp1, "w_cmp2": w_cmp2,
            "cmp_pos": cmp_pos, "sinks": sinks, "w_branch": w_branch, "w_out": w_out,
            "ln_g": ln_g, "ln_b": ln_b}


def reference(x, w_in, b_in, w_cmp1, w_cmp2, cmp_pos, sinks, w_branch, w_out, ln_g, ln_b):
    B, S, _ = x.shape
    split_at = np.cumsum(IN_WIDTHS)[:-1].tolist()
    a_slopes = alibi_slopes(A_GROUPS * A_HEADS).reshape(A_GROUPS, A_HEADS)
    b_slopes = alibi_slopes(B_HEADS)
    c_slopes = alibi_slopes(C_HEADS)
    for l in range(DEPTH):
        h = x @ w_in[l] + b_in[l]
        (aq, ak, av, ag, bq, bck, bcv, bsk, bsv, bwk, bwv, bg, bgate,
         cq, ck, cv, cg, mg) = jnp.split(h, split_at, axis=-1)

        aq = aq.reshape(B, S, A_GROUPS, A_HEADS, HD)
        ak = ak.reshape(B, S, A_GROUPS, A_HEADS, HD)
        av = av.reshape(B, S, A_GROUPS, A_HEADS, HD)
        outs, lses = [], []
        for gi, (win, dil) in enumerate(A_PATTERNS):
            o, lse = dilated_attention(aq[:, :, gi], ak[:, :, gi], av[:, :, gi], a_slopes[gi], win, dil)
            outs.append(o)
            lses.append(lse)
        wts = jax.nn.softmax(jnp.stack(lses), axis=0)
        ya = jnp.sum(wts[..., None] * jnp.stack(outs).astype(jnp.float32), axis=0)
        ya = ya.reshape(B, S, BRANCH_W).astype(x.dtype) * jax.nn.silu(ag)

        bq4 = bq.reshape(B, S, B_HEADS, HD)
        kcmp = nsa_compress(bck.reshape(B, S, B_KV, HD), w_cmp1[l, 0], w_cmp2[l, 0], cmp_pos[l, 0])
        vcmp = nsa_compress(bcv.reshape(B, S, B_KV, HD), w_cmp1[l, 1], w_cmp2[l, 1], cmp_pos[l, 1])
        o_cmp, o_sel = nsa_compressed_and_selected(bq4, kcmp, vcmp, bsk.reshape(B, S, B_KV, HD),
                                                   bsv.reshape(B, S, B_KV, HD), b_slopes)
        o_win, _ = banded_attention(bq4, bwk.reshape(B, S, B_KV, HD), bwv.reshape(B, S, B_KV, HD),
                                    b_slopes, NSA_WINDOW - 1, 1)
        gts = jax.nn.sigmoid(bgate.reshape(B, S, B_HEADS, 3))
        yb = gts[..., 0:1] * o_cmp + gts[..., 1:2] * o_sel + gts[..., 2:3] * o_win
        yb = yb.reshape(B, S, BRANCH_W).astype(x.dtype) * jax.nn.silu(bg)

        o_c, _ = banded_attention(cq.reshape(B, S, C_HEADS, HD), ck.reshape(B, S, C_KV, HD),
                                  cv.reshape(B, S, C_KV, HD), c_slopes, C_WINDOW - 1, 1, sinks=sinks[l])
        yc = o_c.reshape(B, S, BRANCH_W).astype(x.dtype) * jax.nn.silu(cg)

        mg = jax.nn.sigmoid(mg.reshape(B, S, N_BRANCH, D_MODEL))
        merged = (mg[:, :, 0] * (ya @ w_branch[l, 0])
                  + mg[:, :, 1] * (yb @ w_branch[l, 1])
                  + mg[:, :, 2] * (yc @ w_branch[l, 2]))
        y = (merged @ w_out[l]).astype(x.dtype)

        x = layer_norm(DEEPNORM_ALPHA * x + y, ln_g[l], ln_b[l])
    return x
```

```python
import functools
import math

import numpy as np
import jax
import jax.numpy as jnp
from jax import lax
from jax.experimental import pallas as pl
from jax.experimental.pallas import tpu as pltpu

HD = 64
TILE = 128
LANES = 128
N_HEADS = 8
N_KV = 2
GQA = N_HEADS // N_KV
A_PATTERNS = ((128, 1), (512, 4), (2048, 16))
CMP_BLK, CMP_STRIDE, CMP_HIDDEN = 32, 16, 256
SEL_BLK, N_SEL, N_FORCED = 64, 16, 3
NSA_WINDOW, C_WINDOW = 512, 128
LN_EPS = 1e-5
NEG = -1e30
ATTN_SCALE = HD ** -0.5
VMEM_LIMIT = 56 * 1024 * 1024

C_MG, C_AQ0, C_AK0, C_BQ, C_CQ = 0, 3072, 3584, 4096, 4608
C_AG, C_BG, C_CG, C_BGX = 5120, 5632, 6144, 6656
C_BSK, C_BWK, C_CK, C_BCK, C_BCV = 8192, 8448, 8704, 8960, 9088
H1_COLS = 9216
R_AV0, R_BSV, R_BWV, R_CV, V1_ROWS = 0, 512, 640, 768, 896

_BF = jnp.bfloat16
_F32 = jnp.float32
_NT = (((1,), (1,)), ((), ()))


def _cparams(sem):
    return pltpu.CompilerParams(dimension_semantics=sem, vmem_limit_bytes=VMEM_LIMIT)


def _mm_kernel(x_ref, w_ref, b_ref, o_ref):
    acc = jnp.dot(x_ref[...], w_ref[...], preferred_element_type=_F32)
    o_ref[...] = (acc + b_ref[...]).astype(o_ref.dtype)


def _mm_t_kernel(x_ref, wt_ref, bt_ref, o_ref):
    acc = lax.dot_general(wt_ref[...], x_ref[...], _NT, preferred_element_type=_F32)
    o_ref[...] = (acc + bt_ref[...]).astype(o_ref.dtype)


def _matmul(x, w, b, tm=1024, tn=1024):
    m, k = x.shape
    n = w.shape[1]
    tn = min(tn, n)
    return pl.pallas_call(
        _mm_kernel,
        grid=(m // tm, n // tn),
        in_specs=[pl.BlockSpec((tm, k), lambda i, j: (i, 0)),
                  pl.BlockSpec((k, tn), lambda i, j: (0, j)),
                  pl.BlockSpec((1, tn), lambda i, j: (0, j))],
        out_specs=pl.BlockSpec((tm, tn), lambda i, j: (i, j)),
        out_shape=jax.ShapeDtypeStruct((m, n), _BF),
        compiler_params=_cparams(("parallel", "arbitrary")),
        name="proj_tok",
    )(x, w, b)


def _matmul_t(x, wt, bt, tm=1024):
    m, k = x.shape
    n = wt.shape[0]
    return pl.pallas_call(
        _mm_t_kernel,
        grid=(m // tm,),
        in_specs=[pl.BlockSpec((tm, k), lambda i: (i, 0)),
                  pl.BlockSpec((n, k), lambda i: (0, 0)),
                  pl.BlockSpec((n, 1), lambda i: (0, 0))],
        out_specs=pl.BlockSpec((n, tm), lambda i: (0, i)),
        out_shape=jax.ShapeDtypeStruct((n, m), _BF),
        compiler_params=_cparams(("parallel",)),
        name="proj_feat",
    )(x, wt, bt)


def _stack_queries(q_ref, col0, heads):
    lane = lax.broadcasted_iota(jnp.int32, (TILE, LANES), 1)
    parts = []
    for h in heads:
        pair, half = divmod(h, 2)
        qp = q_ref[:, col0 + pair * LANES: col0 + (pair + 1) * LANES]
        keep = (lane >= HD) if half else (lane < HD)
        parts.append(jnp.where(keep, qp, jnp.zeros_like(qp)))
    return parts[0] if len(parts) == 1 else jnp.concatenate(parts, axis=0)


def _softmax_step(state, sb, vt):
    m, l, acc = state
    mn = jnp.maximum(m, jnp.max(sb, axis=0, keepdims=True))
    alpha = jnp.exp(m - mn)
    p = jnp.exp(sb - mn)
    l = alpha * l + jnp.sum(p, axis=0, keepdims=True)
    acc = alpha * acc + jnp.dot(vt, p.astype(_BF), preferred_element_type=_F32)
    return mn, l, acc


def _banded_kernel(*refs, n_delta, units, heads_per_unit, k_lanes, want_lse):
    q_ref = refs[0]
    k_refs = refs[1:1 + n_delta]
    v_refs = refs[1 + n_delta:1 + 2 * n_delta]
    bias_ref, ml_ref = refs[1 + 2 * n_delta:3 + 2 * n_delta]
    outs = refs[3 + 2 * n_delta:]
    o_ref = outs[0]
    lse_ref = outs[1] if want_lse else None
    ot_ref = outs[2] if want_lse else outs[1]
    lt_ref = outs[3] if want_lse else None
    i = pl.program_id(1)
    g = heads_per_unit
    for u in range(units):
        heads = [u * g + j for j in range(g)]
        qs = _stack_queries(q_ref, 0, heads)
        kb = (u // 2) if k_lanes == 4 * LANES else u
        m = ml_ref[u, 0:1, :]
        l = ml_ref[u, 1:2, :]
        acc = jnp.zeros((HD, g * TILE), _F32)
        for d in range(n_delta):
            kt = k_refs[d][:, kb * LANES:(kb + 1) * LANES]
            s = lax.dot_general(kt, qs, _NT, preferred_element_type=_F32)
            sb = s + bias_ref[u, d]
            if d > 0:
                sb = sb + jnp.where(i < d, NEG, 0.0)
            m, l, acc = _softmax_step((m, l, acc), sb, v_refs[d][u * HD:(u + 1) * HD, :])
        o = acc * (1.0 / l)
        lse = m + jnp.log(l)
        for j, h in enumerate(heads):
            ot_ref[h * HD:(h + 1) * HD, :] = o[:, j * TILE:(j + 1) * TILE]
            if want_lse:
                lt_ref[h * HD:(h + 1) * HD, :] = jnp.broadcast_to(lse[:, j * TILE:(j + 1) * TILE], (HD, TILE))
    o_ref[...] = ot_ref[...].T.astype(o_ref.dtype)
    if want_lse:
        lse_ref[...] = lt_ref[...].T


def _banded_bias(slopes, window, dist_scale, n_delta, heads_per_unit):
    kk = np.arange(TILE)[:, None]
    qq = np.arange(TILE)[None, :]
    out = []
    slopes = np.asarray(slopes, np.float64).reshape(-1, heads_per_unit)
    for unit_slopes in slopes:
        per_d = []
        for d in range(n_delta):
            dist = qq - kk + d * TILE
            ok = (dist >= 0) & (dist <= window)
            per_d.append(np.concatenate([np.where(ok, -s * dist_scale * dist, NEG) for s in unit_slopes], axis=1))
        out.append(np.stack(per_d))
    return jnp.asarray(np.stack(out), _F32)


def _banded_attention(q_arr, q_cb, k_arr, k_cb, k_lanes, vt_arr, v_rb, v_rows, bias, ml, nbatch, seq, want_lse):
    ntok = nbatch * seq
    nq = seq // TILE
    units, n_delta = bias.shape[0], bias.shape[1]
    g = N_HEADS // units

    def kmap(d):
        return lambda b, i: (b * nq + jnp.maximum(i - d, 0), k_cb)

    def vmap_(d):
        return lambda b, i: (v_rb, b * nq + jnp.maximum(i - d, 0))

    in_specs = [pl.BlockSpec((TILE, 4 * LANES), lambda b, i: (b * nq + i, q_cb))]
    in_specs += [pl.BlockSpec((TILE, k_lanes), kmap(d)) for d in range(n_delta)]
    in_specs += [pl.BlockSpec((v_rows, TILE), vmap_(d)) for d in range(n_delta)]
    in_specs += [pl.BlockSpec(bias.shape, lambda b, i: (0, 0, 0, 0)),
                 pl.BlockSpec(ml.shape, lambda b, i: (0, 0, 0))]
    out_spec = pl.BlockSpec((TILE, 4 * LANES), lambda b, i: (b * nq + i, 0))
    out_shape = [jax.ShapeDtypeStruct((ntok, 4 * LANES), _BF)]
    out_specs = [out_spec]
    scratch = [pltpu.VMEM((4 * LANES, TILE), _F32)]
    if want_lse:
        out_shape.append(jax.ShapeDtypeStruct((ntok, 4 * LANES), _F32))
        out_specs.append(out_spec)
        scratch.append(pltpu.VMEM((4 * LANES, TILE), _F32))
    kern = functools.partial(_banded_kernel, n_delta=n_delta, units=units, heads_per_unit=g,
                             k_lanes=k_lanes, want_lse=want_lse)
    res = pl.pallas_call(
        kern, grid=(nbatch, nq), in_specs=in_specs, out_specs=out_specs, out_shape=out_shape,
        scratch_shapes=scratch, compiler_params=_cparams(("parallel", "arbitrary")),
        name=f"banded_u{units}_d{n_delta}",
    )(q_arr, *([k_arr] * n_delta), *([vt_arr] * n_delta), bias, ml)
    return res if want_lse else res[0]


def _compress_kernel(cf_ref, w1_ref, ptop_ref, pbot_ref, w2d_ref, w2t_ref, kc_ref, vct_ref):
    nchunk = cf_ref.shape[0]
    half = CMP_STRIDE * HD
    c = cf_ref[...].astype(_F32)
    top = jnp.dot((c + ptop_ref[...]).astype(_BF), w1_ref[0:half, :], preferred_element_type=_F32)
    bot = jnp.dot((c + pbot_ref[...]).astype(_BF), w1_ref[half:2 * half, :], preferred_element_type=_F32)
    hid = top + pltpu.roll(bot, nchunk - 1, 0)
    act = jax.nn.gelu(hid, approximate=True).astype(_BF)
    kc_ref[...] = jnp.dot(act, w2d_ref[...], preferred_element_type=_F32).astype(kc_ref.dtype)
    vct_ref[...] = lax.dot_general(w2t_ref[...], act, _NT, preferred_element_type=_F32).astype(vct_ref.dtype)


def _compress(cf, w1, ptop, pbot, w2d, w2t):
    two, nb, nkv, nchunk, width = cf.shape
    sq = pl.Squeezed()
    return pl.pallas_call(
        _compress_kernel,
        grid=(two, nb, nkv),
        in_specs=[pl.BlockSpec((sq, sq, sq, nchunk, width), lambda t, b, h: (t, b, h, 0, 0)),
                  pl.BlockSpec((sq, 2 * width, CMP_HIDDEN), lambda t, b, h: (t, 0, 0)),
                  pl.BlockSpec((sq, 1, width), lambda t, b, h: (t, 0, 0)),
                  pl.BlockSpec((sq, 1, width), lambda t, b, h: (t, 0, 0)),
                  pl.BlockSpec((sq, CMP_HIDDEN, 2 * HD), lambda t, b, h: (t, 0, 0)),
                  pl.BlockSpec((sq, HD, CMP_HIDDEN), lambda t, b, h: (t, 0, 0))],
        out_specs=[pl.BlockSpec((sq, sq, sq, nchunk, 2 * HD), lambda t, b, h: (t, b, h, 0, 0)),
                   pl.BlockSpec((sq, sq, sq, HD, nchunk), lambda t, b, h: (t, b, h, 0, 0))],
        out_shape=[jax.ShapeDtypeStruct((two, nb, nkv, nchunk, 2 * HD), _BF),
                   jax.ShapeDtypeStruct((two, nb, nkv, HD, nchunk), _BF)],
        compiler_params=_cparams(("parallel", "parallel", "parallel")),
        name="nsa_compress",
    )(cf, w1, ptop, pbot, w2d, w2t)


def _select_kernel(q_ref, kc_ref, vct_ref, ovt_ref, d0_ref, srow_ref, ocmp_ref, sel_ref, cnt_ref, ot_ref):
    hkv = pl.program_id(1)
    i = pl.program_id(2)
    t0 = (i * TILE).astype(_F32)
    qs = _stack_queries(q_ref, 0, list(range(GQA)))
    s = lax.dot_general(kc_ref[...], qs, _NT, preferred_element_type=_F32)
    dist = d0_ref[...] + t0
    valid = dist >= 0
    sb = jnp.where(valid, s - srow_ref[hkv] * dist, NEG)
    m = jnp.max(sb, axis=0, keepdims=True)
    e = jnp.where(valid, jnp.exp(sb - m), 0.0)
    den = jnp.sum(e, axis=0, keepdims=True)
    p = e * (1.0 / jnp.where(den > 0, den, 1.0))
    o = jnp.dot(vct_ref[...], p.astype(_BF), preferred_element_type=_F32)
    for j in range(GQA):
        ot_ref[j * HD:(j + 1) * HD, :] = o[:, j * TILE:(j + 1) * TILE]
    ocmp_ref[...] = ot_ref[...].T.astype(ocmp_ref.dtype)
    psum = p[:, 0:TILE]
    for j in range(1, GQA):
        psum = psum + p[:, j * TILE:(j + 1) * TILE]
    hi = psum.astype(_BF)
    lo = (psum - hi.astype(_F32)).astype(_BF)
    imp = (jnp.dot(ovt_ref[...], hi, preferred_element_type=_F32)
           + jnp.dot(ovt_ref[...], lo, preferred_element_type=_F32))
    nsel = imp.shape[0]
    jf = lax.broadcasted_iota(jnp.int32, (nsel, TILE), 0).astype(_F32)
    qlane = lax.broadcasted_iota(jnp.int32, (nsel, TILE), 1)
    cur = (i * (TILE // SEL_BLK)).astype(_F32) + jnp.where(qlane >= SEL_BLK, 1.0, 0.0)
    forced = (jf == 0.0) | (jf == cur) | (jf == cur - 1.0)
    free = (jf <= cur) & jnp.logical_not(forced)
    work0 = jnp.where(free, imp, -2.0)
    sel0 = jnp.where(forced, 1.0, 0.0)

    def pick(_, carry):
        work, sel = carry
        mx = jnp.max(work, axis=0, keepdims=True)
        first = jnp.min(jnp.where(work == mx, jf, 1e9), axis=0, keepdims=True)
        hit = (jf == first) & (mx >= 0.0)
        return jnp.where(hit, -2.0, work), jnp.where(hit, 1.0, sel)

    _, sel = lax.fori_loop(0, N_SEL - N_FORCED, pick, (work0, sel0))
    sel_ref[...] = sel
    ones = jnp.ones((8, TILE), _BF)
    cnt_ref[...] = lax.dot_general(ones, sel.astype(_BF), _NT, preferred_element_type=_F32)


def _nsa_select(h1, kc, vct, ovt, d0, srow, nb, seq):
    nq = seq // TILE
    ncmp = kc.shape[2]
    nsel = seq // SEL_BLK
    sq = pl.Squeezed()
    qcb = C_BQ // (2 * LANES)
    return pl.pallas_call(
        _select_kernel,
        grid=(nb, N_KV, nq),
        in_specs=[pl.BlockSpec((TILE, 2 * LANES), lambda b, h, i: (b * nq + i, qcb + h)),
                  pl.BlockSpec((sq, sq, ncmp, 2 * HD), lambda b, h, i: (b, h, 0, 0)),
                  pl.BlockSpec((sq, sq, HD, ncmp), lambda b, h, i: (b, h, 0, 0)),
                  pl.BlockSpec((nsel, ncmp), lambda b, h, i: (0, 0)),
                  pl.BlockSpec((ncmp, GQA * TILE), lambda b, h, i: (0, 0)),
                  pl.BlockSpec((N_KV, 1, GQA * TILE), lambda b, h, i: (0, 0, 0))],
        out_specs=[pl.BlockSpec((TILE, 2 * LANES), lambda b, h, i: (b * nq + i, h)),
                   pl.BlockSpec((sq, sq, sq, nsel, TILE), lambda b, h, i: (b, h, i, 0, 0)),
                   pl.BlockSpec((sq, sq, sq, 8, nsel), lambda b, h, i: (b, h, i, 0, 0))],
        out_shape=[jax.ShapeDtypeStruct((nb * seq, 4 * LANES), _BF),
                   jax.ShapeDtypeStruct((nb, N_KV, nq, nsel, TILE), _F32),
                   jax.ShapeDtypeStruct((nb, N_KV, nq, 8, nsel), _F32)],
        scratch_shapes=[pltpu.VMEM((2 * LANES, TILE), _F32)],
        compiler_params=_cparams(("parallel", "parallel", "arbitrary")),
        name="nsa_select",
    )(h1, kc, vct, ovt, d0, srow)


def _selected_kernel(flag_ref, q_ref, k_ref, vt_ref, sel_ref, cdiff_ref, dmask_ref, srow_ref, o_ref, ot_ref, *, nq):
    b = pl.program_id(0)
    hkv = pl.program_id(1)
    i = pl.program_id(2)
    ntile = sel_ref.shape[0] // 2
    fbase = ((b * N_KV + hkv) * nq + i) * ntile
    qs = _stack_queries(q_ref, 0, list(range(GQA)))
    n = GQA * TILE
    srow = srow_ref[hkv]
    cdiff = cdiff_ref[hkv]
    sub = lax.broadcasted_iota(jnp.int32, (8, TILE), 0)

    def scores(jj):
        kt = k_ref[pl.ds(pl.multiple_of(jj * TILE, TILE), TILE), :]
        return lax.dot_general(kt, qs, _NT, preferred_element_type=_F32) + cdiff

    def values(jj):
        return vt_ref[:, pl.ds(pl.multiple_of(jj * TILE, TILE), TILE)]

    def sel_row(blk8, r):
        row = jnp.max(jnp.where(sub == r, blk8, 0.0), axis=0, keepdims=True)
        return jnp.concatenate([row] * GQA, axis=1)

    def tile_update(jj, state):
        base8 = pl.multiple_of((jj // 4) * 8, 8)
        blk8 = sel_ref[pl.ds(base8, 8), :]
        r = (jj % 4) * 2
        off = -srow * ((i - jj) * TILE).astype(_F32)
        r0 = jnp.where(sel_row(blk8, r) > 0, off, NEG)
        r1 = jnp.where(sel_row(blk8, r + 1) > 0, off, NEG)
        rowbias = jnp.concatenate([jnp.broadcast_to(r0, (SEL_BLK, n)), jnp.broadcast_to(r1, (SEL_BLK, n))], axis=0)
        return _softmax_step(state, scores(jj) + rowbias, values(jj))

    def body(jj, state):
        return lax.cond(flag_ref[fbase + jj] > 0, lambda st: tile_update(jj, st), lambda st: st, state)

    init = (jnp.full((1, n), NEG, _F32), jnp.zeros((1, n), _F32), jnp.zeros((HD, n), _F32))
    state = lax.fori_loop(0, i, body, init)
    m, l, acc = _softmax_step(state, scores(i) + dmask_ref[...], values(i))
    o = acc * (1.0 / l)
    for j in range(GQA):
        ot_ref[j * HD:(j + 1) * HD, :] = o[:, j * TILE:(j + 1) * TILE]
    o_ref[...] = ot_ref[...].T.astype(o_ref.dtype)


def _nsa_selected(flags, h1, v1t, sel, cdiff, dmask, srow, nb, seq):
    nq = seq // TILE
    nsel = seq // SEL_BLK
    sq = pl.Squeezed()
    qcb = C_BQ // (2 * LANES)
    kcb = C_BSK // LANES
    vrb = R_BSV // HD
    grid_spec = pltpu.PrefetchScalarGridSpec(
        num_scalar_prefetch=1,
        grid=(nb, N_KV, nq),
        in_specs=[pl.BlockSpec((TILE, 2 * LANES), lambda b, h, i, f: (b * nq + i, qcb + h)),
                  pl.BlockSpec((seq, LANES), lambda b, h, i, f: (b, kcb + h)),
                  pl.BlockSpec((HD, seq), lambda b, h, i, f: (vrb + h, b)),
                  pl.BlockSpec((sq, sq, sq, nsel, TILE), lambda b, h, i, f: (b, h, i, 0, 0)),
                  pl.BlockSpec((N_KV, TILE, GQA * TILE), lambda b, h, i, f: (0, 0, 0)),
                  pl.BlockSpec((TILE, GQA * TILE), lambda b, h, i, f: (0, 0)),
                  pl.BlockSpec((N_KV, 1, GQA * TILE), lambda b, h, i, f: (0, 0, 0))],
        out_specs=pl.BlockSpec((TILE, 2 * LANES), lambda b, h, i, f: (b * nq + i, h)),
        scratch_shapes=[pltpu.VMEM((2 * LANES, TILE), _F32)],
    )
    return pl.pallas_call(
        functools.partial(_selected_kernel, nq=nq),
        grid_spec=grid_spec,
        out_shape=jax.ShapeDtypeStruct((nb * seq, 4 * LANES), _BF),
        compiler_params=_cparams(("parallel", "parallel", "arbitrary")),
        name="nsa_selected",
    )(flags, h1, h1, v1t, sel, cdiff, dmask, srow)


def _merge_kernel(oa0, oa1, oa2, la0, la1, la2, ocmp, osel, owin, oc, ag, bg, cg, gx0, gx1, gx2, mg0, mg1, mg2,
                  x_ref, wb_ref, wo_ref, lng_ref, lnb_ref, xo_ref, xb_ref, *, alpha):
    f = lambda r: r[...].astype(_F32)
    sig = jax.nn.sigmoid
    silu = lambda v: v * sig(v)
    l0, l1, l2 = la0[...], la1[...], la2[...]
    mx = jnp.maximum(jnp.maximum(l0, l1), l2)
    e0, e1, e2 = jnp.exp(l0 - mx), jnp.exp(l1 - mx), jnp.exp(l2 - mx)
    ya = (e0 * f(oa0) + e1 * f(oa1) + e2 * f(oa2)) * (1.0 / (e0 + e1 + e2))
    ya = ya * silu(f(ag))
    yb = sig(f(gx0)) * f(ocmp) + sig(f(gx1)) * f(osel) + sig(f(gx2)) * f(owin)
    yb = yb * silu(f(bg))
    yc = f(oc) * silu(f(cg))
    merged = (sig(f(mg0)) * jnp.dot(ya.astype(_BF), wb_ref[0], preferred_element_type=_F32)
              + sig(f(mg1)) * jnp.dot(yb.astype(_BF), wb_ref[1], preferred_element_type=_F32)
              + sig(f(mg2)) * jnp.dot(yc.astype(_BF), wb_ref[2], preferred_element_type=_F32))
    y = jnp.dot(merged.astype(_BF), wo_ref[...], preferred_element_type=_F32)
    z = alpha * x_ref[...] + y
    mu = jnp.mean(z, axis=-1, keepdims=True)
    zc = z - mu
    var = jnp.mean(zc * zc, axis=-1, keepdims=True)
    out = zc * lax.rsqrt(var + LN_EPS) * lng_ref[...] + lnb_ref[...]
    xo_ref[...] = out
    xb_ref[...] = out.astype(_BF)


def _merge(oa, la, ocmp, osel, owin, oc, h1, x, wb, wo, lng, lnb, alpha, tm=256):
    ntok, dm = x.shape
    w = 4 * LANES
    tok = lambda c: pl.BlockSpec((tm, w), lambda i: (i, c))
    h1s = lambda col, width: pl.BlockSpec((tm, width), lambda i: (i, col // width))
    in_specs = ([tok(0)] * 3 + [tok(0)] * 3 + [tok(0)] * 4
                + [h1s(C_AG, w), h1s(C_BG, w), h1s(C_CG, w)]
                + [h1s(C_BGX + j * w, w) for j in range(3)]
                + [h1s(C_MG + j * dm, dm) for j in range(3)]
                + [pl.BlockSpec((tm, dm), lambda i: (i, 0)),
                   pl.BlockSpec(wb.shape, lambda i: (0, 0, 0)),
                   pl.BlockSpec(wo.shape, lambda i: (0, 0)),
                   pl.BlockSpec((1, dm), lambda i: (0, 0)),
                   pl.BlockSpec((1, dm), lambda i: (0, 0))])
    return pl.pallas_call(
        functools.partial(_merge_kernel, alpha=alpha),
        grid=(ntok // tm,),
        in_specs=in_specs,
        out_specs=[pl.BlockSpec((tm, dm), lambda i: (i, 0))] * 2,
        out_shape=[jax.ShapeDtypeStruct((ntok, dm), _F32), jax.ShapeDtypeStruct((ntok, dm), _BF)],
        compiler_params=_cparams(("parallel",)),
        name="merge_norm",
    )(*oa, *la, ocmp, osel, owin, oc, *([h1] * 9), x, wb, wo, lng, lnb)


def _alibi(n):
    return 2.0 ** (-8.0 * np.arange(1, n + 1, dtype=np.float64) / n)


def _fold(a, nb, seq, r):
    return a.reshape(nb, seq // r, r, a.shape[-1]).transpose(0, 2, 1, 3).reshape(nb * seq, a.shape[-1])


def _unfold(a, nb, seq, r):
    return a.reshape(nb, r, seq // r, a.shape[-1]).transpose(0, 2, 1, 3).reshape(nb * seq, a.shape[-1])


def _prep_weights(w_in, b_in, w_cmp1, w_cmp2, cmp_pos, w_branch, w_out):
    widths = (1536, 1536, 1536, 512, 512, 128, 128, 128, 128, 128, 128, 512, 24, 512, 128, 128, 512, 3072)
    names = ("aq", "ak", "av", "ag", "bq", "bck", "bcv", "bsk", "bsv", "bwk", "bwv", "bg", "bgate", "cq", "ck", "cv", "cg", "mg")
    off = dict(zip(names, np.cumsum((0,) + widths[:-1]).tolist()))

    def cols(idx, scale=1.0):
        idx = np.asarray(idx)
        return w_in[:, :, idx] * scale, b_in[:, idx] * scale

    def rng(name, start, n):
        return np.arange(off[name] + start, off[name] + start + n)

    def dup(name):
        return np.concatenate([rng(name, kv * HD, HD) for kv in range(N_KV) for _ in range(2)])

    gate_cols = [off["bgate"] + (np.arange(N_HEADS * HD) // HD) * 3 + j for j in range(3)]
    tok = [cols(rng("mg", 0, 3072)), cols(rng("aq", 0, 512), ATTN_SCALE), cols(rng("ak", 0, 512)),
           cols(rng("bq", 0, 512), ATTN_SCALE), cols(rng("cq", 0, 512), ATTN_SCALE),
           cols(rng("ag", 0, 512)), cols(rng("bg", 0, 512)), cols(rng("cg", 0, 512)),
           cols(np.concatenate(gate_cols)), cols(dup("bsk")), cols(dup("bwk")), cols(dup("ck")),
           cols(rng("bck", 0, 128)), cols(rng("bcv", 0, 128))]
    w1 = jnp.concatenate([t[0] for t in tok], axis=2).astype(_BF)
    b1 = jnp.concatenate([t[1] for t in tok], axis=1)[:, None, :]
    wqk, bqk, wvt, bvt = [], [], [], []
    for gi in (1, 2):
        wq, bq = cols(rng("aq", gi * 512, 512), ATTN_SCALE)
        wk, bk = cols(rng("ak", gi * 512, 512))
        wqk.append(jnp.concatenate([wq, wk], axis=2).astype(_BF))
        bqk.append(jnp.concatenate([bq, bk], axis=1)[:, None, :])
        wv, bv = cols(rng("av", gi * 512, 512))
        wvt.append(wv.transpose(0, 2, 1).astype(_BF))
        bvt.append(bv[:, :, None])
    feat = [cols(rng("av", 0, 512)), cols(rng("bsv", 0, 128)), cols(rng("bwv", 0, 128)), cols(rng("cv", 0, 128))]
    wv1t = jnp.concatenate([t[0] for t in feat], axis=2).transpose(0, 2, 1).astype(_BF)
    bv1t = jnp.concatenate([t[1] for t in feat], axis=1)[:, :, None]
    half = CMP_STRIDE * HD
    posf = cmp_pos.reshape(cmp_pos.shape[0], 2, 1, CMP_BLK * HD)
    cmpw = dict(w1=w_cmp1.astype(_BF), ptop=posf[..., :half], pbot=posf[..., half:],
                w2d=jnp.concatenate([w_cmp2, w_cmp2], axis=-1).astype(_BF),
                w2t=w_cmp2.transpose(0, 1, 3, 2).astype(_BF))
    return dict(w1=w1, b1=b1, wqk=wqk, bqk=bqk, wvt=wvt, bvt=bvt, wv1t=wv1t, bv1t=bv1t, cmp=cmpw,
                wb=w_branch.astype(_BF), wo=w_out.astype(_BF))


def _constants(seq, sinks):
    a_slopes = _alibi(3 * N_HEADS).reshape(3, N_HEADS)
    bc_slopes = _alibi(N_HEADS)
    no_sink = lambda units, n: jnp.stack([jnp.full((units, n), NEG, _F32), jnp.zeros((units, n), _F32)], axis=1)
    c = {}
    c["a_bias"] = [_banded_bias(a_slopes[gi], win // dil, dil, 2, 1) for gi, (win, dil) in enumerate(A_PATTERNS)]
    c["a_ml"] = no_sink(N_HEADS, TILE)
    c["bw_bias"] = _banded_bias(bc_slopes, NSA_WINDOW - 1, 1, 1 + NSA_WINDOW // TILE, GQA)
    c["bw_ml"] = no_sink(N_KV, GQA * TILE)
    c["c_bias"] = _banded_bias(bc_slopes, C_WINDOW - 1, 1, 2, GQA)
    sink_rows = jnp.repeat(sinks.astype(_F32).reshape(-1, N_KV, GQA), TILE, axis=-1)
    c["c_ml"] = jnp.stack([sink_rows, jnp.ones_like(sink_rows)], axis=2)
    lane_slopes = np.repeat(bc_slopes.reshape(N_KV, GQA), TILE, axis=-1)
    c["srow"] = jnp.asarray(lane_slopes[:, None, :], _F32)
    ncmp = seq // CMP_STRIDE
    nsel = seq // SEL_BLK
    cs = np.arange(ncmp) * CMP_STRIDE
    ss = np.arange(nsel) * SEL_BLK
    overlap = (cs[None, :] < ss[:, None] + SEL_BLK) & (cs[None, :] + CMP_BLK > ss[:, None]) & (cs[None, :] + CMP_BLK <= seq)
    c["ovt"] = jnp.asarray(overlap, _BF)
    qq = np.tile(np.arange(TILE), GQA)[None, :]
    c["d0"] = jnp.asarray(qq - (cs[:, None] + CMP_BLK - 1), _F32)
    kk = np.arange(TILE)[:, None]
    c["cdiff"] = jnp.asarray(-lane_slopes[:, None, :] * (qq - kk)[None], _F32)
    c["dmask"] = jnp.asarray(np.where(kk <= qq, 0.0, NEG), _F32)
    return c


def kernel(x, w_in, b_in, w_cmp1, w_cmp2, cmp_pos, sinks, w_branch, w_out, ln_g, ln_b):
    nb, seq, dm = x.shape
    depth = w_in.shape[0]
    ntok = nb * seq
    alpha = (2 * depth) ** 0.25
    wts = _prep_weights(w_in, b_in, w_cmp1, w_cmp2, cmp_pos, w_branch, w_out)
    cst = _constants(seq, sinks)
    nchunk = seq // CMP_STRIDE
    nq = seq // TILE
    xf = x.reshape(ntok, dm)
    xb = xf.astype(_BF)
    for l in range(depth):
        h1 = _matmul(xb, wts["w1"][l], wts["b1"][l])
        v1t = _matmul_t(xb, wts["wv1t"][l], wts["bv1t"][l])
        oa, la = [], []
        for gi, (win, dil) in enumerate(A_PATTERNS):
            if dil == 1:
                qk, qcb, kcb, vt = h1, C_AQ0 // 512, C_AK0 // 512, v1t
            else:
                xfold = _fold(xb, nb, seq, dil)
                qk = _matmul(xfold, wts["wqk"][gi - 1][l], wts["bqk"][gi - 1][l])
                vt = _matmul_t(xfold, wts["wvt"][gi - 1][l], wts["bvt"][gi - 1][l])
                qcb, kcb = 0, 1
            o, lse = _banded_attention(qk, qcb, qk, kcb, 4 * LANES, vt, R_AV0 // 512, 4 * LANES,
                                       cst["a_bias"][gi], cst["a_ml"], nb * dil, seq // dil, True)
            if dil > 1:
                o, lse = _unfold(o, nb, seq, dil), _unfold(lse, nb, seq, dil)
            oa.append(o)
            la.append(lse)
        cf = h1[:, C_BCK:C_BCK + 2 * LANES].reshape(nb, nchunk, CMP_STRIDE, 2, N_KV, HD)
        cf = cf.transpose(3, 0, 4, 1, 2, 5).reshape(2, nb, N_KV, nchunk, CMP_STRIDE * HD)
        cw = wts["cmp"]
        kc, vct = _compress(cf, cw["w1"][l], cw["ptop"][l], cw["pbot"][l], cw["w2d"][l], cw["w2t"][l])
        ocmp, sel, cnt = _nsa_select(h1, kc[0], vct[1], cst["ovt"], cst["d0"], cst["srow"], nb, seq)
        flags = (cnt[:, :, :, 0, :].reshape(nb, N_KV, nq, nq, 2).sum(-1) > 0).astype(jnp.int32).reshape(-1)
        osel = _nsa_selected(flags, h1, v1t, sel, cst["cdiff"], cst["dmask"], cst["srow"], nb, seq)
        owin = _banded_attention(h1, C_BQ // 512, h1, C_BWK // (2 * LANES), 2 * LANES, v1t, R_BWV // LANES, LANES,
                                 cst["bw_bias"], cst["bw_ml"], nb, seq, False)
        oc = _banded_attention(h1, C_CQ // 512, h1, C_CK // (2 * LANES), 2 * LANES, v1t, R_CV // LANES, LANES,
                               cst["c_bias"], cst["c_ml"][l], nb, seq, False)
        xf, xb = _merge(oa, la, ocmp, osel, owin, oc, h1, xf, wts["wb"][l], wts["wo"][l],
                        ln_g[l][None, :], ln_b[l][None, :], alpha)
    return xf.reshape(nb, seq, dm).astype(x.dtype)
```

```python
import functools
import math

import numpy as np
import jax
import jax.numpy as jnp
from jax import lax
from jax.experimental import pallas as pl
from jax.experimental.pallas import tpu as pltpu

HD = 64
TILE = 128
LANES = 128
N_HEADS = 8
N_KV = 2
GQA = N_HEADS // N_KV
A_PATTERNS = ((128, 1), (512, 4), (2048, 16))
CMP_BLK, CMP_STRIDE, CMP_HIDDEN = 32, 16, 256
SEL_BLK, N_SEL, N_FORCED = 64, 16, 3
NSA_WINDOW, C_WINDOW = 512, 128
SEL_WIDE = 4
LN_EPS = 1e-5
NEG = -1e30
ATTN_SCALE = HD ** -0.5
VMEM_LIMIT = 56 * 1024 * 1024

C_MG, C_AQ0, C_AK0, C_BQ, C_CQ = 0, 3072, 3584, 4096, 4608
C_AG, C_BG, C_CG, C_BGX = 5120, 5632, 6144, 6656
C_BSK, C_BWK, C_CK, C_BCK, C_BCV = 8192, 8448, 8704, 8960, 9088
H1_COLS = 9216
R_AV0, R_BSV, R_BWV, R_CV, V1_ROWS = 0, 512, 640, 768, 896

_BF = jnp.bfloat16
_F32 = jnp.float32
_NT = (((1,), (1,)), ((), ()))


def _cparams(sem):
    return pltpu.CompilerParams(dimension_semantics=sem, vmem_limit_bytes=VMEM_LIMIT)


def _mm_kernel(x_ref, w_ref, b_ref, o_ref):
    acc = jnp.dot(x_ref[...], w_ref[...], preferred_element_type=_F32)
    o_ref[...] = (acc + b_ref[...]).astype(o_ref.dtype)


def _mm_t_kernel(x_ref, wt_ref, bt_ref, o_ref):
    acc = lax.dot_general(wt_ref[...], x_ref[...], _NT, preferred_element_type=_F32)
    o_ref[...] = (acc + bt_ref[...]).astype(o_ref.dtype)


def _matmul(x, w, b, tm=1024, tn=1024):
    m, k = x.shape
    n = w.shape[1]
    tn = min(tn, n)
    return pl.pallas_call(
        _mm_kernel,
        grid=(m // tm, n // tn),
        in_specs=[pl.BlockSpec((tm, k), lambda i, j: (i, 0)),
                  pl.BlockSpec((k, tn), lambda i, j: (0, j)),
                  pl.BlockSpec((1, tn), lambda i, j: (0, j))],
        out_specs=pl.BlockSpec((tm, tn), lambda i, j: (i, j)),
        out_shape=jax.ShapeDtypeStruct((m, n), _BF),
        compiler_params=_cparams(("parallel", "arbitrary")),
        name="proj_tok",
    )(x, w, b)


def _matmul_t(x, wt, bt, tm=1024):
    m, k = x.shape
    n = wt.shape[0]
    return pl.pallas_call(
        _mm_t_kernel,
        grid=(m // tm,),
        in_specs=[pl.BlockSpec((tm, k), lambda i: (i, 0)),
                  pl.BlockSpec((n, k), lambda i: (0, 0)),
                  pl.BlockSpec((n, 1), lambda i: (0, 0))],
        out_specs=pl.BlockSpec((n, tm), lambda i: (0, i)),
        out_shape=jax.ShapeDtypeStruct((n, m), _BF),
        compiler_params=_cparams(("parallel",)),
        name="proj_feat",
    )(x, wt, bt)


def _stack_queries(q_ref, col0, heads):
    lane = lax.broadcasted_iota(jnp.int32, (TILE, LANES), 1)
    parts = []
    for h in heads:
        pair, half = divmod(h, 2)
        qp = q_ref[:, col0 + pair * LANES: col0 + (pair + 1) * LANES]
        keep = (lane >= HD) if half else (lane < HD)
        parts.append(jnp.where(keep, qp, jnp.zeros_like(qp)))
    return parts[0] if len(parts) == 1 else jnp.concatenate(parts, axis=0)


def _softmax_step(state, sb, vt):
    m, l, acc = state
    mn = jnp.maximum(m, jnp.max(sb, axis=0, keepdims=True))
    alpha = jnp.exp(m - mn)
    p = jnp.exp(sb - mn)
    l = alpha * l + jnp.sum(p, axis=0, keepdims=True)
    acc = alpha * acc + jnp.dot(vt, p.astype(_BF), preferred_element_type=_F32)
    return mn, l, acc


def _run_skewed(unit_stages):
    nstage = max(len(s) for s in unit_stages)
    for t in range(len(unit_stages) + nstage - 1):
        for u in range(min(t, len(unit_stages) - 1), -1, -1):
            if t - u < len(unit_stages[u]):
                unit_stages[u][t - u]()


def _banded_kernel(*refs, n_delta, units, heads_per_unit, shared_values, want_lse):
    q_ref = refs[0]
    k_refs = refs[1:1 + n_delta]
    v_refs = refs[1 + n_delta:1 + 2 * n_delta]
    bias_ref, ml_ref = refs[1 + 2 * n_delta:3 + 2 * n_delta]
    outs = refs[3 + 2 * n_delta:]
    o_ref = outs[0]
    lse_ref = outs[1] if want_lse else None
    ot_ref = outs[2] if want_lse else outs[1]
    lt_ref = outs[3] if want_lse else None
    i = pl.program_id(1)
    g = heads_per_unit
    vr = HD if shared_values else g * HD

    def unit(u):
        st = {}
        heads = [u * g + j for j in range(g)]

        def scores():
            qs = _stack_queries(q_ref, 0, heads)
            ss = []
            for d in range(n_delta):
                kt = k_refs[d][:, u * LANES:(u + 1) * LANES]
                s = lax.dot_general(kt, qs, _NT, preferred_element_type=_F32) + bias_ref[u, d]
                if d > 0:
                    s = s + jnp.where(i < d, NEG, 0.0)
                ss.append(s)
            st["s"] = ss

        def softmax():
            ss = st.pop("s")
            m0 = ml_ref[u, 0:1, :]
            m = m0
            for s in ss:
                m = jnp.maximum(m, jnp.max(s, axis=0, keepdims=True))
            l = ml_ref[u, 1:2, :] * jnp.exp(m0 - m)
            ps = []
            for s in ss:
                p = jnp.exp(s - m)
                l = l + jnp.sum(p, axis=0, keepdims=True)
                ps.append(p.astype(_BF))
            st["p"], st["m"], st["l"] = ps, m, l

        def values():
            ps, m, l = st.pop("p"), st.pop("m"), st.pop("l")
            acc = None
            for d in range(n_delta):
                part = jnp.dot(v_refs[d][u * vr:(u + 1) * vr, :], ps[d], preferred_element_type=_F32)
                acc = part if acc is None else acc + part
            o = acc * (1.0 / l)
            lse = m + jnp.log(l)
            for j, h in enumerate(heads):
                r0 = 0 if shared_values else j * HD
                ot_ref[h * HD:(h + 1) * HD, :] = o[r0:r0 + HD, j * TILE:(j + 1) * TILE]
                if want_lse:
                    lt_ref[h * HD:(h + 1) * HD, :] = jnp.broadcast_to(lse[:, j * TILE:(j + 1) * TILE], (HD, TILE))

        return [scores, softmax, values]

    _run_skewed([unit(u) for u in range(units)])
    o_ref[...] = ot_ref[...].T.astype(o_ref.dtype)
    if want_lse:
        lse_ref[...] = lt_ref[...].T


def _banded_bias(slopes, window, dist_scale, n_delta, heads_per_unit):
    kk = np.arange(TILE)[:, None]
    qq = np.arange(TILE)[None, :]
    out = []
    slopes = np.asarray(slopes, np.float64).reshape(-1, heads_per_unit)
    for unit_slopes in slopes:
        per_d = []
        for d in range(n_delta):
            dist = qq - kk + d * TILE
            ok = (dist >= 0) & (dist <= window)
            per_d.append(np.concatenate([np.where(ok, -s * dist_scale * dist, NEG) for s in unit_slopes], axis=1))
        out.append(np.stack(per_d))
    return jnp.asarray(np.stack(out), _F32)


def _banded_attention(q_arr, q_cb, k_arr, k_cb, k_lanes, vt_arr, v_rb, v_rows, bias, ml, nbatch, seq, want_lse):
    ntok = nbatch * seq
    nq = seq // TILE
    units, n_delta = bias.shape[0], bias.shape[1]
    g = N_HEADS // units

    def kmap(d):
        return lambda b, i: (b * nq + jnp.maximum(i - d, 0), k_cb)

    def vmap_(d):
        return lambda b, i: (v_rb, b * nq + jnp.maximum(i - d, 0))

    in_specs = [pl.BlockSpec((TILE, 4 * LANES), lambda b, i: (b * nq + i, q_cb))]
    in_specs += [pl.BlockSpec((TILE, k_lanes), kmap(d)) for d in range(n_delta)]
    in_specs += [pl.BlockSpec((v_rows, TILE), vmap_(d)) for d in range(n_delta)]
    in_specs += [pl.BlockSpec(bias.shape, lambda b, i: (0, 0, 0, 0)),
                 pl.BlockSpec(ml.shape, lambda b, i: (0, 0, 0))]
    out_spec = pl.BlockSpec((TILE, 4 * LANES), lambda b, i: (b * nq + i, 0))
    out_shape = [jax.ShapeDtypeStruct((ntok, 4 * LANES), _BF)]
    out_specs = [out_spec]
    scratch = [pltpu.VMEM((4 * LANES, TILE), _F32)]
    if want_lse:
        out_shape.append(jax.ShapeDtypeStruct((ntok, 4 * LANES), _F32))
        out_specs.append(out_spec)
        scratch.append(pltpu.VMEM((4 * LANES, TILE), _F32))
    kern = functools.partial(_banded_kernel, n_delta=n_delta, units=units, heads_per_unit=g,
                             shared_values=v_rows == units * HD, want_lse=want_lse)
    res = pl.pallas_call(
        kern, grid=(nbatch, nq), in_specs=in_specs, out_specs=out_specs, out_shape=out_shape,
        scratch_shapes=scratch, compiler_params=_cparams(("parallel", "arbitrary")),
        name=f"banded_u{units}_d{n_delta}",
    )(q_arr, *([k_arr] * n_delta), *([vt_arr] * n_delta), bias, ml)
    return res if want_lse else res[0]


def _compress_kernel(cf_ref, w1_ref, ptop_ref, pbot_ref, w2d_ref, w2t_ref, kc_ref, vct_ref):
    nchunk = cf_ref.shape[0]
    half = CMP_STRIDE * HD
    c = cf_ref[...].astype(_F32)
    top = jnp.dot((c + ptop_ref[...]).astype(_BF), w1_ref[0:half, :], preferred_element_type=_F32)
    bot = jnp.dot((c + pbot_ref[...]).astype(_BF), w1_ref[half:2 * half, :], preferred_element_type=_F32)
    hid = top + pltpu.roll(bot, nchunk - 1, 0)
    act = jax.nn.gelu(hid, approximate=True).astype(_BF)
    kc_ref[...] = jnp.dot(act, w2d_ref[...], preferred_element_type=_F32).astype(kc_ref.dtype)
    vct_ref[...] = lax.dot_general(w2t_ref[...], act, _NT, preferred_element_type=_F32).astype(vct_ref.dtype)


def _compress(cf, w1, ptop, pbot, w2d, w2t):
    two, nb, nkv, nchunk, width = cf.shape
    sq = pl.Squeezed()
    return pl.pallas_call(
        _compress_kernel,
        grid=(two, nb, nkv),
        in_specs=[pl.BlockSpec((sq, sq, sq, nchunk, width), lambda t, b, h: (t, b, h, 0, 0)),
                  pl.BlockSpec((sq, 2 * width, CMP_HIDDEN), lambda t, b, h: (t, 0, 0)),
                  pl.BlockSpec((sq, 1, width), lambda t, b, h: (t, 0, 0)),
                  pl.BlockSpec((sq, 1, width), lambda t, b, h: (t, 0, 0)),
                  pl.BlockSpec((sq, CMP_HIDDEN, 2 * HD), lambda t, b, h: (t, 0, 0)),
                  pl.BlockSpec((sq, HD, CMP_HIDDEN), lambda t, b, h: (t, 0, 0))],
        out_specs=[pl.BlockSpec((sq, sq, sq, nchunk, 2 * HD), lambda t, b, h: (t, b, h, 0, 0)),
                   pl.BlockSpec((sq, sq, sq, HD, nchunk), lambda t, b, h: (t, b, h, 0, 0))],
        out_shape=[jax.ShapeDtypeStruct((two, nb, nkv, nchunk, 2 * HD), _BF),
                   jax.ShapeDtypeStruct((two, nb, nkv, HD, nchunk), _BF)],
        compiler_params=_cparams(("parallel", "parallel", "parallel")),
        name="nsa_compress",
    )(cf, w1, ptop, pbot, w2d, w2t)


def _select_kernel(q_ref, kc_ref, vct_ref, ovt_ref, d0_ref, srow_ref, ocmp_ref, sel_ref, cnt_ref, ot_ref):
    hkv = pl.program_id(1)
    i = pl.program_id(2)
    t0 = (i * TILE).astype(_F32)
    qs = _stack_queries(q_ref, 0, list(range(GQA)))
    s = lax.dot_general(kc_ref[...], qs, _NT, preferred_element_type=_F32)
    dist = d0_ref[...] + t0
    valid = dist >= 0
    sb = jnp.where(valid, s - srow_ref[hkv] * dist, NEG)
    m = jnp.max(sb, axis=0, keepdims=True)
    e = jnp.where(valid, jnp.exp(sb - m), 0.0)
    den = jnp.sum(e, axis=0, keepdims=True)
    p = e * (1.0 / jnp.where(den > 0, den, 1.0))
    o = jnp.dot(vct_ref[...], p.astype(_BF), preferred_element_type=_F32)
    for j in range(GQA):
        ot_ref[j * HD:(j + 1) * HD, :] = o[:, j * TILE:(j + 1) * TILE]
    ocmp_ref[...] = ot_ref[...].T.astype(ocmp_ref.dtype)
    psum = p[:, 0:TILE]
    for j in range(1, GQA):
        psum = psum + p[:, j * TILE:(j + 1) * TILE]
    hi = psum.astype(_BF)
    lo = (psum - hi.astype(_F32)).astype(_BF)
    imp = (jnp.dot(ovt_ref[...], hi, preferred_element_type=_F32)
           + jnp.dot(ovt_ref[...], lo, preferred_element_type=_F32))
    nsel = imp.shape[0]
    jf = lax.broadcasted_iota(jnp.int32, (nsel, TILE), 0).astype(_F32)
    qlane = lax.broadcasted_iota(jnp.int32, (nsel, TILE), 1)
    cur = (i * (TILE // SEL_BLK)).astype(_F32) + jnp.where(qlane >= SEL_BLK, 1.0, 0.0)
    forced = (jf == 0.0) | (jf == cur) | (jf == cur - 1.0)
    free = (jf <= cur) & jnp.logical_not(forced)
    work0 = jnp.where(free, imp, -2.0)
    sel0 = jnp.where(forced, 1.0, 0.0)

    def pick(_, carry):
        work, sel = carry
        mx = jnp.max(work, axis=0, keepdims=True)
        first = jnp.min(jnp.where(work == mx, jf, 1e9), axis=0, keepdims=True)
        hit = (jf == first) & (mx >= 0.0)
        return jnp.where(hit, -2.0, work), jnp.where(hit, 1.0, sel)

    _, sel = lax.fori_loop(0, N_SEL - N_FORCED, pick, (work0, sel0))
    sel_ref[...] = sel
    ones = jnp.ones((8, TILE), _BF)
    cnt_ref[...] = lax.dot_general(ones, sel.astype(_BF), _NT, preferred_element_type=_F32)


def _nsa_select(h1, kc, vct, ovt, d0, srow, nb, seq):
    nq = seq // TILE
    ncmp = kc.shape[2]
    nsel = seq // SEL_BLK
    sq = pl.Squeezed()
    qcb = C_BQ // (2 * LANES)
    return pl.pallas_call(
        _select_kernel,
        grid=(nb, N_KV, nq),
        in_specs=[pl.BlockSpec((TILE, 2 * LANES), lambda b, h, i: (b * nq + i, qcb + h)),
                  pl.BlockSpec((sq, sq, ncmp, 2 * HD), lambda b, h, i: (b, h, 0, 0)),
                  pl.BlockSpec((sq, sq, HD, ncmp), lambda b, h, i: (b, h, 0, 0)),
                  pl.BlockSpec((nsel, ncmp), lambda b, h, i: (0, 0)),
                  pl.BlockSpec((ncmp, GQA * TILE), lambda b, h, i: (0, 0)),
                  pl.BlockSpec((N_KV, 1, GQA * TILE), lambda b, h, i: (0, 0, 0))],
        out_specs=[pl.BlockSpec((TILE, 2 * LANES), lambda b, h, i: (b * nq + i, h)),
                   pl.BlockSpec((sq, sq, sq, nsel, TILE), lambda b, h, i: (b, h, i, 0, 0)),
                   pl.BlockSpec((sq, sq, sq, 8, nsel), lambda b, h, i: (b, h, i, 0, 0))],
        out_shape=[jax.ShapeDtypeStruct((nb * seq, 4 * LANES), _BF),
                   jax.ShapeDtypeStruct((nb, N_KV, nq, nsel, TILE), _F32),
                   jax.ShapeDtypeStruct((nb, N_KV, nq, 8, nsel), _F32)],
        scratch_shapes=[pltpu.VMEM((2 * LANES, TILE), _F32)],
        compiler_params=_cparams(("parallel", "parallel", "arbitrary")),
        name="nsa_select",
    )(h1, kc, vct, ovt, d0, srow)


def _selected_kernel(flag_ref, q_ref, k_ref, vt_ref, sel_ref, cdiff_ref, dmask_ref, srow_ref, o_ref, ot_ref, list_ref,
                     *, nq):
    b = pl.program_id(0)
    hkv = pl.program_id(1)
    i = pl.program_id(2)
    fbase = ((b * N_KV + hkv) * nq + i) * nq
    qs = _stack_queries(q_ref, 0, list(range(GQA)))
    n = GQA * TILE
    srow = srow_ref[hkv]
    cdiff = cdiff_ref[hkv]
    sub = lax.broadcasted_iota(jnp.int32, (8, TILE), 0)

    def scan(jj, cnt):
        list_ref[cnt] = jj
        return cnt + (flag_ref[fbase + jj] > 0).astype(jnp.int32)

    n_act = lax.fori_loop(0, i, scan, jnp.int32(0))
    for w in range(SEL_WIDE):
        list_ref[n_act + w] = 0

    def sel_row(blk8, r):
        row = jnp.max(jnp.where(sub == r, blk8, 0.0), axis=0, keepdims=True)
        return jnp.concatenate([row] * GQA, axis=1)

    def list_scores(idx):
        jj = list_ref[idx]
        ok = idx < n_act
        kt = k_ref[pl.ds(pl.multiple_of(jj * TILE, TILE), TILE), :]
        s = lax.dot_general(kt, qs, _NT, preferred_element_type=_F32) + cdiff
        blk8 = sel_ref[pl.ds(pl.multiple_of((jj // 4) * 8, 8), 8), :]
        r = jnp.where(ok, (jj % 4) * 2, -2)
        off = -srow * ((i - jj) * TILE).astype(_F32)
        r0 = jnp.where(sel_row(blk8, r) > 0, off, NEG)
        r1 = jnp.where(sel_row(blk8, r + 1) > 0, off, NEG)
        rowbias = jnp.concatenate([jnp.broadcast_to(r0, (SEL_BLK, n)), jnp.broadcast_to(r1, (SEL_BLK, n))], axis=0)
        return s + rowbias, jj

    def update(state, tiles):
        m, l, acc = state
        mn = m
        for s, _ in tiles:
            mn = jnp.maximum(mn, jnp.max(s, axis=0, keepdims=True))
        alpha = jnp.exp(m - mn)
        l = alpha * l
        pv = None
        for s, jj in tiles:
            p = jnp.exp(s - mn)
            l = l + jnp.sum(p, axis=0, keepdims=True)
            vt = vt_ref[:, pl.ds(pl.multiple_of(jj * TILE, TILE), TILE)]
            part = jnp.dot(vt, p.astype(_BF), preferred_element_type=_F32)
            pv = part if pv is None else pv + part
        return mn, l, alpha * acc + pv

    kt = k_ref[pl.ds(pl.multiple_of(i * TILE, TILE), TILE), :]
    diag = lax.dot_general(kt, qs, _NT, preferred_element_type=_F32) + cdiff + dmask_ref[...]
    init = (jnp.full((1, n), NEG, _F32), jnp.zeros((1, n), _F32), jnp.zeros((HD, n), _F32))
    state = update(init, [(diag, i)] + [list_scores(w) for w in range(SEL_WIDE - 1)])

    def body(t, st):
        base = SEL_WIDE - 1 + t * SEL_WIDE
        return update(st, [list_scores(base + w) for w in range(SEL_WIDE)])

    ngroups = (jnp.maximum(n_act - (SEL_WIDE - 1), 0) + SEL_WIDE - 1) // SEL_WIDE
    m, l, acc = lax.fori_loop(0, ngroups, body, state)
    o = acc * (1.0 / l)
    for j in range(GQA):
        ot_ref[j * HD:(j + 1) * HD, :] = o[:, j * TILE:(j + 1) * TILE]
    o_ref[...] = ot_ref[...].T.astype(o_ref.dtype)


def _nsa_selected(flags, h1, v1t, sel, cdiff, dmask, srow, nb, seq):
    nq = seq // TILE
    nsel = seq // SEL_BLK
    sq = pl.Squeezed()
    qcb = C_BQ // (2 * LANES)
    kcb = C_BSK // LANES
    vrb = R_BSV // HD
    grid_spec = pltpu.PrefetchScalarGridSpec(
        num_scalar_prefetch=1,
        grid=(nb, N_KV, nq),
        in_specs=[pl.BlockSpec((TILE, 2 * LANES), lambda b, h, i, f: (b * nq + i, qcb + h)),
                  pl.BlockSpec((seq, LANES), lambda b, h, i, f: (b, kcb + h)),
                  pl.BlockSpec((HD, seq), lambda b, h, i, f: (vrb + h, b)),
                  pl.BlockSpec((sq, sq, sq, nsel, TILE), lambda b, h, i, f: (b, h, i, 0, 0)),
                  pl.BlockSpec((N_KV, TILE, GQA * TILE), lambda b, h, i, f: (0, 0, 0)),
                  pl.BlockSpec((TILE, GQA * TILE), lambda b, h, i, f: (0, 0)),
                  pl.BlockSpec((N_KV, 1, GQA * TILE), lambda b, h, i, f: (0, 0, 0))],
        out_specs=pl.BlockSpec((TILE, 2 * LANES), lambda b, h, i, f: (b * nq + i, h)),
        scratch_shapes=[pltpu.VMEM((2 * LANES, TILE), _F32), pltpu.SMEM((nq + 2 * SEL_WIDE,), jnp.int32)],
    )
    return pl.pallas_call(
        functools.partial(_selected_kernel, nq=nq),
        grid_spec=grid_spec,
        out_shape=jax.ShapeDtypeStruct((nb * seq, 4 * LANES), _BF),
        compiler_params=_cparams(("parallel", "parallel", "arbitrary")),
        name="nsa_selected",
    )(flags, h1, h1, v1t, sel, cdiff, dmask, srow)


def _merge_kernel(oa0, oa1, oa2, la0, la1, la2, ocmp, osel, owin, oc, ag, bg, cg, gx0, gx1, gx2, mg0, mg1, mg2,
                  x_ref, wb_ref, wo_ref, lng_ref, lnb_ref, xo_ref, xb_ref, *, alpha):
    f = lambda r: r[...].astype(_F32)
    sig = jax.nn.sigmoid
    silu = lambda v: v * sig(v)
    l0, l1, l2 = la0[...], la1[...], la2[...]
    mx = jnp.maximum(jnp.maximum(l0, l1), l2)
    e0, e1, e2 = jnp.exp(l0 - mx), jnp.exp(l1 - mx), jnp.exp(l2 - mx)
    ya = (e0 * f(oa0) + e1 * f(oa1) + e2 * f(oa2)) * (1.0 / (e0 + e1 + e2))
    ya = ya * silu(f(ag))
    yb = sig(f(gx0)) * f(ocmp) + sig(f(gx1)) * f(osel) + sig(f(gx2)) * f(owin)
    yb = yb * silu(f(bg))
    yc = f(oc) * silu(f(cg))
    merged = (sig(f(mg0)) * jnp.dot(ya.astype(_BF), wb_ref[0], preferred_element_type=_F32)
              + sig(f(mg1)) * jnp.dot(yb.astype(_BF), wb_ref[1], preferred_element_type=_F32)
              + sig(f(mg2)) * jnp.dot(yc.astype(_BF), wb_ref[2], preferred_element_type=_F32))
    y = jnp.dot(merged.astype(_BF), wo_ref[...], preferred_element_type=_F32)
    z = alpha * x_ref[...] + y
    mu = jnp.mean(z, axis=-1, keepdims=True)
    zc = z - mu
    var = jnp.mean(zc * zc, axis=-1, keepdims=True)
    out = zc * lax.rsqrt(var + LN_EPS) * lng_ref[...] + lnb_ref[...]
    xo_ref[...] = out
    xb_ref[...] = out.astype(_BF)


def _merge(oa, la, ocmp, osel, owin, oc, h1, x, wb, wo, lng, lnb, alpha, tm=256):
    ntok, dm = x.shape
    w = 4 * LANES
    tok = lambda c: pl.BlockSpec((tm, w), lambda i: (i, c))
    h1s = lambda col, width: pl.BlockSpec((tm, width), lambda i: (i, col // width))
    in_specs = ([tok(0)] * 3 + [tok(0)] * 3 + [tok(0)] * 4
                + [h1s(C_AG, w), h1s(C_BG, w), h1s(C_CG, w)]
                + [h1s(C_BGX + j * w, w) for j in range(3)]
                + [h1s(C_MG + j * dm, dm) for j in range(3)]
                + [pl.BlockSpec((tm, dm), lambda i: (i, 0)),
                   pl.BlockSpec(wb.shape, lambda i: (0, 0, 0)),
                   pl.BlockSpec(wo.shape, lambda i: (0, 0)),
                   pl.BlockSpec((1, dm), lambda i: (0, 0)),
                   pl.BlockSpec((1, dm), lambda i: (0, 0))])
    return pl.pallas_call(
        functools.partial(_merge_kernel, alpha=alpha),
        grid=(ntok // tm,),
        in_specs=in_specs,
        out_specs=[pl.BlockSpec((tm, dm), lambda i: (i, 0))] * 2,
        out_shape=[jax.ShapeDtypeStruct((ntok, dm), _F32), jax.ShapeDtypeStruct((ntok, dm), _BF)],
        compiler_params=_cparams(("parallel",)),
        name="merge_norm",
    )(*oa, *la, ocmp, osel, owin, oc, *([h1] * 9), x, wb, wo, lng, lnb)


def _alibi(n):
    return 2.0 ** (-8.0 * np.arange(1, n + 1, dtype=np.float64) / n)


def _fold(a, nb, seq, r):
    return a.reshape(nb, seq // r, r, a.shape[-1]).transpose(0, 2, 1, 3).reshape(nb * seq, a.shape[-1])


def _unfold(a, nb, seq, r):
    return a.reshape(nb, r, seq // r, a.shape[-1]).transpose(0, 2, 1, 3).reshape(nb * seq, a.shape[-1])


def _prep_weights(w_in, b_in, w_cmp1, w_cmp2, cmp_pos, w_branch, w_out):
    widths = (1536, 1536, 1536, 512, 512, 128, 128, 128, 128, 128, 128, 512, 24, 512, 128, 128, 512, 3072)
    names = ("aq", "ak", "av", "ag", "bq", "bck", "bcv", "bsk", "bsv", "bwk", "bwv", "bg", "bgate", "cq", "ck", "cv", "cg", "mg")
    off = dict(zip(names, np.cumsum((0,) + widths[:-1]).tolist()))

    def cols(sl, scale=1.0):
        w, b = w_in[:, :, sl], b_in[:, sl]
        return (w, b) if scale == 1.0 else (w * scale, b * scale)

    def rng(name, start, n):
        return slice(off[name] + start, off[name] + start + n)

    def cat(parts):
        return jnp.concatenate([p[0] for p in parts], axis=2), jnp.concatenate([p[1] for p in parts], axis=1)

    def dup(name):
        return cat([cols(rng(name, kv * HD, HD)) for kv in range(N_KV) for _ in range(2)])

    def gate(j):
        w, b = cols(slice(off["bgate"] + j, off["bgate"] + 3 * N_HEADS, 3))
        return jnp.repeat(w, HD, axis=2), jnp.repeat(b, HD, axis=1)

    tok = [cols(rng("mg", 0, 3072)), cols(rng("aq", 0, 512), ATTN_SCALE), cols(rng("ak", 0, 512)),
           cols(rng("bq", 0, 512), ATTN_SCALE), cols(rng("cq", 0, 512), ATTN_SCALE),
           cols(rng("ag", 0, 512)), cols(rng("bg", 0, 512)), cols(rng("cg", 0, 512)),
           gate(0), gate(1), gate(2), dup("bsk"), dup("bwk"), dup("ck"),
           cols(rng("bck", 0, 128)), cols(rng("bcv", 0, 128))]
    w1 = jnp.concatenate([t[0].astype(_BF) for t in tok], axis=2)
    b1 = jnp.concatenate([t[1] for t in tok], axis=1)[:, None, :]
    wqk, bqk, wvt, bvt = [], [], [], []
    for gi in (1, 2):
        wq, bq = cols(rng("aq", gi * 512, 512), ATTN_SCALE)
        wk, bk = cols(rng("ak", gi * 512, 512))
        wqk.append(jnp.concatenate([wq, wk], axis=2).astype(_BF))
        bqk.append(jnp.concatenate([bq, bk], axis=1)[:, None, :])
        wv, bv = cols(rng("av", gi * 512, 512))
        wvt.append(wv.transpose(0, 2, 1).astype(_BF))
        bvt.append(bv[:, :, None])
    feat = [cols(rng("av", 0, 512)), cols(rng("bsv", 0, 128)), cols(rng("bwv", 0, 128)), cols(rng("cv", 0, 128))]
    wv1t = jnp.concatenate([t[0] for t in feat], axis=2).transpose(0, 2, 1).astype(_BF)
    bv1t = jnp.concatenate([t[1] for t in feat], axis=1)[:, :, None]
    half = CMP_STRIDE * HD
    posf = cmp_pos.reshape(cmp_pos.shape[0], 2, 1, CMP_BLK * HD)
    cmpw = dict(w1=w_cmp1.astype(_BF), ptop=posf[..., :half], pbot=posf[..., half:],
                w2d=jnp.concatenate([w_cmp2, w_cmp2], axis=-1).astype(_BF),
                w2t=w_cmp2.transpose(0, 1, 3, 2).astype(_BF))
    return dict(w1=w1, b1=b1, wqk=wqk, bqk=bqk, wvt=wvt, bvt=bvt, wv1t=wv1t, bv1t=bv1t, cmp=cmpw,
                wb=w_branch.astype(_BF), wo=w_out.astype(_BF))


def _constants(seq, sinks):
    a_slopes = _alibi(3 * N_HEADS).reshape(3, N_HEADS)
    bc_slopes = _alibi(N_HEADS)
    no_sink = lambda units, n: jnp.stack([jnp.full((units, n), NEG, _F32), jnp.zeros((units, n), _F32)], axis=1)
    c = {}
    c["a_bias"] = [_banded_bias(a_slopes[gi], win // dil, dil, 2, 2) for gi, (win, dil) in enumerate(A_PATTERNS)]
    c["a_ml"] = no_sink(N_HEADS // 2, 2 * TILE)
    c["bw_bias"] = _banded_bias(bc_slopes, NSA_WINDOW - 1, 1, 1 + NSA_WINDOW // TILE, GQA)
    c["bw_ml"] = no_sink(N_KV, GQA * TILE)
    c["c_bias"] = _banded_bias(bc_slopes, C_WINDOW - 1, 1, 2, GQA)
    sink_rows = jnp.repeat(sinks.astype(_F32).reshape(-1, N_KV, GQA), TILE, axis=-1)
    c["c_ml"] = jnp.stack([sink_rows, jnp.ones_like(sink_rows)], axis=2)
    lane_slopes = np.repeat(bc_slopes.reshape(N_KV, GQA), TILE, axis=-1)
    c["srow"] = jnp.asarray(lane_slopes[:, None, :], _F32)
    ncmp = seq // CMP_STRIDE
    nsel = seq // SEL_BLK
    cs = np.arange(ncmp) * CMP_STRIDE
    ss = np.arange(nsel) * SEL_BLK
    overlap = (cs[None, :] < ss[:, None] + SEL_BLK) & (cs[None, :] + CMP_BLK > ss[:, None]) & (cs[None, :] + CMP_BLK <= seq)
    c["ovt"] = jnp.asarray(overlap, _BF)
    qq = np.tile(np.arange(TILE), GQA)[None, :]
    c["d0"] = jnp.asarray(qq - (cs[:, None] + CMP_BLK - 1), _F32)
    kk = np.arange(TILE)[:, None]
    c["cdiff"] = jnp.asarray(-lane_slopes[:, None, :] * (qq - kk)[None], _F32)
    c["dmask"] = jnp.asarray(np.where(kk <= qq, 0.0, NEG), _F32)
    return c


def kernel(x, w_in, b_in, w_cmp1, w_cmp2, cmp_pos, sinks, w_branch, w_out, ln_g, ln_b):
    nb, seq, dm = x.shape
    depth = w_in.shape[0]
    ntok = nb * seq
    alpha = (2 * depth) ** 0.25
    wts = _prep_weights(w_in, b_in, w_cmp1, w_cmp2, cmp_pos, w_branch, w_out)
    cst = _constants(seq, sinks)
    nchunk = seq // CMP_STRIDE
    nq = seq // TILE
    xf = x.reshape(ntok, dm)
    xb = xf.astype(_BF)
    for l in range(depth):
        h1 = _matmul(xb, wts["w1"][l], wts["b1"][l])
        v1t = _matmul_t(xb, wts["wv1t"][l], wts["bv1t"][l])
        oa, la = [], []
        for gi, (win, dil) in enumerate(A_PATTERNS):
            if dil == 1:
                qk, qcb, kcb, vt = h1, C_AQ0 // 512, C_AK0 // 512, v1t
            else:
                xfold = _fold(xb, nb, seq, dil)
                qk = _matmul(xfold, wts["wqk"][gi - 1][l], wts["bqk"][gi - 1][l])
                vt = _matmul_t(xfold, wts["wvt"][gi - 1][l], wts["bvt"][gi - 1][l])
                qcb, kcb = 0, 1
            o, lse = _banded_attention(qk, qcb, qk, kcb, 4 * LANES, vt, R_AV0 // 512, 4 * LANES,
                                       cst["a_bias"][gi], cst["a_ml"], nb * dil, seq // dil, True)
            if dil > 1:
                o, lse = _unfold(o, nb, seq, dil), _unfold(lse, nb, seq, dil)
            oa.append(o)
            la.append(lse)
        cf = h1[:, C_BCK:C_BCK + 2 * LANES].reshape(nb, nchunk, CMP_STRIDE, 2, N_KV, HD)
        cf = cf.transpose(3, 0, 4, 1, 2, 5).reshape(2, nb, N_KV, nchunk, CMP_STRIDE * HD)
        cw = wts["cmp"]
        kc, vct = _compress(cf, cw["w1"][l], cw["ptop"][l], cw["pbot"][l], cw["w2d"][l], cw["w2t"][l])
        ocmp, sel, cnt = _nsa_select(h1, kc[0], vct[1], cst["ovt"], cst["d0"], cst["srow"], nb, seq)
        flags = (cnt[:, :, :, 0, :].reshape(nb, N_KV, nq, nq, 2).sum(-1) > 0).astype(jnp.int32).reshape(-1)
        osel = _nsa_selected(flags, h1, v1t, sel, cst["cdiff"], cst["dmask"], cst["srow"], nb, seq)
        owin = _banded_attention(h1, C_BQ // 512, h1, C_BWK // (2 * LANES), 2 * LANES, v1t, R_BWV // LANES, LANES,
                                 cst["bw_bias"], cst["bw_ml"], nb, seq, False)
        oc = _banded_attention(h1, C_CQ // 512, h1, C_CK // (2 * LANES), 2 * LANES, v1t, R_CV // LANES, LANES,
                               cst["c_bias"], cst["c_ml"][l], nb, seq, False)
        xf, xb = _merge(oa, la, ocmp, osel, owin, oc, h1, xf, wts["wb"][l], wts["wo"][l],
                        ln_g[l][None, :], ln_b[l][None, :], alpha)
    return xf.reshape(nb, seq, dm).astype(x.dtype)
```

```python
import functools
import math

import ml_dtypes
import numpy as np
import jax
import jax.numpy as jnp
from jax import lax
from jax.experimental import pallas as pl
from jax.experimental.pallas import tpu as pltpu

HD = 64
TILE = 128
LANES = 128
N_HEADS = 8
N_KV = 2
GQA = N_HEADS // N_KV
A_PATTERNS = ((128, 1), (512, 4), (2048, 16))
CMP_BLK, CMP_STRIDE, CMP_HIDDEN = 32, 16, 256
SEL_BLK, N_SEL, N_FORCED = 64, 16, 3
NSA_WINDOW, C_WINDOW = 512, 128
SEL_WIDE = 4
LN_EPS = 1e-5
NEG = -1e30
ATTN_SCALE = HD ** -0.5
LOG2E = math.log2(math.e)
LN2 = math.log(2.0)
Q_SCALE = ATTN_SCALE * LOG2E
VMEM_LIMIT = 56 * 1024 * 1024

C_MG, C_AQ0, C_AK0, C_BQ, C_CQ = 0, 3072, 3584, 4096, 4608
C_AG, C_BG, C_CG = 5120, 5632, 6144
C_BSK, C_BWK, C_CK, C_BCK, C_BCV, C_BGATE = 6656, 6912, 7168, 7424, 7552, 7680
H1_COLS = 8192
R_AV0, R_BSV, R_BWV, R_CV, V1_ROWS = 0, 512, 640, 768, 896

_BF = jnp.bfloat16
_F32 = jnp.float32
_NT = (((1,), (1,)), ((), ()))


def _cparams(sem):
    return pltpu.CompilerParams(dimension_semantics=sem, vmem_limit_bytes=VMEM_LIMIT)


def _mm_kernel(x_ref, w_ref, b_ref, o_ref):
    acc = jnp.dot(x_ref[...], w_ref[...], preferred_element_type=_F32)
    o_ref[...] = (acc + b_ref[...]).astype(o_ref.dtype)


def _mm_t_kernel(x_ref, wt_ref, bt_ref, o_ref):
    acc = lax.dot_general(wt_ref[...], x_ref[...], _NT, preferred_element_type=_F32)
    o_ref[...] = (acc + bt_ref[...]).astype(o_ref.dtype)


def _matmul(x, w, b, nclass=1, tm=1024, tn=1024):
    m = x.shape[0]
    k, n = w.shape
    tm, tn = min(tm, m), min(tn, n)
    nj = n // tn
    return pl.pallas_call(
        _mm_kernel,
        grid=(nclass, m // tm, nj),
        in_specs=[pl.BlockSpec((tm, k), lambda c, i, j: (i, c)),
                  pl.BlockSpec((k, tn), lambda c, i, j: (0, j)),
                  pl.BlockSpec((1, tn), lambda c, i, j: (0, j))],
        out_specs=pl.BlockSpec((tm, tn), lambda c, i, j: (i, c * nj + j)),
        out_shape=jax.ShapeDtypeStruct((m, nclass * n), _BF),
        compiler_params=_cparams(("parallel", "parallel", "arbitrary")),
        name="proj_tok",
    )(x, w, b)


def _matmul_t(x, wt, bt, nclass=1, tm=1024):
    m = x.shape[0]
    n, k = wt.shape
    tm = min(tm, m)
    return pl.pallas_call(
        _mm_t_kernel,
        grid=(nclass, m // tm),
        in_specs=[pl.BlockSpec((tm, k), lambda c, i: (i, c)),
                  pl.BlockSpec((n, k), lambda c, i: (0, 0)),
                  pl.BlockSpec((n, 1), lambda c, i: (0, 0))],
        out_specs=pl.BlockSpec((pl.Squeezed(), n, tm), lambda c, i: (c, 0, i)),
        out_shape=jax.ShapeDtypeStruct((nclass, n, m), _BF),
        compiler_params=_cparams(("parallel", "parallel")),
        name="proj_feat",
    )(x, wt, bt)


def _stack_queries(q_ref, col0, heads):
    lane = lax.broadcasted_iota(jnp.int32, (TILE, LANES), 1)
    parts = []
    for h in heads:
        pair, half = divmod(h, 2)
        qp = q_ref[:, col0 + pair * LANES: col0 + (pair + 1) * LANES]
        keep = (lane >= HD) if half else (lane < HD)
        parts.append(jnp.where(keep, qp, jnp.zeros_like(qp)))
    return parts[0] if len(parts) == 1 else jnp.concatenate(parts, axis=0)


def _run_skewed(unit_stages):
    nstage = max(len(s) for s in unit_stages)
    for t in range(len(unit_stages) + nstage - 1):
        for u in range(min(t, len(unit_stages) - 1), -1, -1):
            if t - u < len(unit_stages[u]):
                unit_stages[u][t - u]()


def _banded_kernel(*refs, n_delta, units, heads_per_unit, shared_values, want_lse):
    q_ref = refs[0]
    k_refs = refs[1:1 + n_delta]
    v_refs = refs[1 + n_delta:1 + 2 * n_delta]
    bias_ref, ml_ref = refs[1 + 2 * n_delta:3 + 2 * n_delta]
    outs = refs[3 + 2 * n_delta:]
    o_ref = outs[0]
    lse_ref = outs[1] if want_lse else None
    ot_ref = outs[2] if want_lse else outs[1]
    lt_ref = outs[3] if want_lse else None
    i = pl.program_id(2)
    g = heads_per_unit
    vr = HD if shared_values else g * HD

    def unit(u):
        st = {}
        heads = [u * g + j for j in range(g)]

        def scores():
            qs = _stack_queries(q_ref, 0, heads)
            ss = []
            for d in range(n_delta):
                kt = k_refs[d][:, u * LANES:(u + 1) * LANES]
                s = lax.dot_general(kt, qs, _NT, preferred_element_type=_F32) + bias_ref[u, d]
                if d > 0:
                    s = s + jnp.where(i < d, NEG, 0.0)
                ss.append(s)
            st["s"] = ss

        def softmax():
            ss = st.pop("s")
            m0 = ml_ref[u, 0:1, :]
            m = m0
            for s in ss:
                m = jnp.maximum(m, jnp.max(s, axis=0, keepdims=True))
            l = ml_ref[u, 1:2, :] * jnp.exp2(m0 - m)
            ps = []
            for s in ss:
                p = jnp.exp2(s - m)
                l = l + jnp.sum(p, axis=0, keepdims=True)
                ps.append(p.astype(_BF))
            st["p"], st["m"], st["l"] = ps, m, l

        def values():
            ps, m, l = st.pop("p"), st.pop("m"), st.pop("l")
            acc = None
            for d in range(n_delta):
                part = jnp.dot(v_refs[d][u * vr:(u + 1) * vr, :], ps[d], preferred_element_type=_F32)
                acc = part if acc is None else acc + part
            o = acc * (1.0 / l)
            lse = (m + jnp.log2(l)) * LN2
            for j, h in enumerate(heads):
                r0 = 0 if shared_values else j * HD
                ot_ref[h * HD:(h + 1) * HD, :] = o[r0:r0 + HD, j * TILE:(j + 1) * TILE]
                if want_lse:
                    lt_ref[h * HD:(h + 1) * HD, :] = jnp.broadcast_to(lse[:, j * TILE:(j + 1) * TILE], (HD, TILE))

        return [scores, softmax, values]

    _run_skewed([unit(u) for u in range(units)])
    o_ref[...] = ot_ref[...].T.astype(o_ref.dtype)
    if want_lse:
        lse_ref[...] = lt_ref[...].T


def _banded_bias(slopes, window, dist_scale, n_delta, heads_per_unit):
    kk = np.arange(TILE)[:, None]
    qq = np.arange(TILE)[None, :]
    out = []
    slopes = np.asarray(slopes, np.float64).reshape(-1, heads_per_unit)
    for unit_slopes in slopes:
        per_d = []
        for d in range(n_delta):
            dist = qq - kk + d * TILE
            ok = (dist >= 0) & (dist <= window)
            per_d.append(np.concatenate([np.where(ok, -s * LOG2E * dist_scale * dist, NEG) for s in unit_slopes], axis=1))
        out.append(np.stack(per_d))
    return jnp.asarray(np.stack(out), _F32)


def _banded_attention(q_arr, q_cb, k_arr, k_cb, k_lanes, vt_arr, v_rb, v_rows, bias, ml, nbatch, seq, want_lse,
                      nclass=1):
    nrow = nbatch * seq
    nq = seq // TILE
    units, n_delta = bias.shape[0], bias.shape[1]
    g = N_HEADS // units
    cs = 2 if nclass > 1 else 0

    def kmap(d):
        return lambda b, c, i: (b * nq + jnp.maximum(i - d, 0), k_cb + cs * c)

    def vmap_(d):
        return lambda b, c, i: (c, v_rb, b * nq + jnp.maximum(i - d, 0))

    in_specs = [pl.BlockSpec((TILE, 4 * LANES), lambda b, c, i: (b * nq + i, q_cb + cs * c))]
    in_specs += [pl.BlockSpec((TILE, k_lanes), kmap(d)) for d in range(n_delta)]
    in_specs += [pl.BlockSpec((pl.Squeezed(), v_rows, TILE), vmap_(d)) for d in range(n_delta)]
    in_specs += [pl.BlockSpec(bias.shape, lambda b, c, i: (0, 0, 0, 0)),
                 pl.BlockSpec(ml.shape, lambda b, c, i: (0, 0, 0))]
    out_spec = pl.BlockSpec((TILE, 4 * LANES), lambda b, c, i: (b * nq + i, c))
    out_shape = [jax.ShapeDtypeStruct((nrow, nclass * 4 * LANES), _BF)]
    out_specs = [out_spec]
    scratch = [pltpu.VMEM((4 * LANES, TILE), _F32)]
    if want_lse:
        out_shape.append(jax.ShapeDtypeStruct((nrow, nclass * 4 * LANES), _F32))
        out_specs.append(out_spec)
        scratch.append(pltpu.VMEM((4 * LANES, TILE), _F32))
    kern = functools.partial(_banded_kernel, n_delta=n_delta, units=units, heads_per_unit=g,
                             shared_values=v_rows == units * HD, want_lse=want_lse)
    res = pl.pallas_call(
        kern, grid=(nbatch, nclass, nq), in_specs=in_specs, out_specs=out_specs, out_shape=out_shape,
        scratch_shapes=scratch, compiler_params=_cparams(("parallel", "parallel", "arbitrary")),
        name=f"banded_u{units}_d{n_delta}",
    )(q_arr, *([k_arr] * n_delta), *([vt_arr] * n_delta), bias, ml)
    return res if want_lse else res[0]


def _compress_kernel(cf_ref, w1_ref, ptop_ref, pbot_ref, w2d_ref, w2t_ref, kc_ref, vct_ref):
    nchunk = cf_ref.shape[0]
    half = CMP_STRIDE * HD
    c = cf_ref[...].astype(_F32)
    top = jnp.dot((c + ptop_ref[...]).astype(_BF), w1_ref[0:half, :], preferred_element_type=_F32)
    bot = jnp.dot((c + pbot_ref[...]).astype(_BF), w1_ref[half:2 * half, :], preferred_element_type=_F32)
    hid = top + pltpu.roll(bot, nchunk - 1, 0)
    act = jax.nn.gelu(hid, approximate=True).astype(_BF)
    kc_ref[...] = jnp.dot(act, w2d_ref[...], preferred_element_type=_F32).astype(kc_ref.dtype)
    vct_ref[...] = lax.dot_general(w2t_ref[...], act, _NT, preferred_element_type=_F32).astype(vct_ref.dtype)


def _compress(cf, w1, ptop, pbot, w2d, w2t):
    two, nb, nkv, nchunk, width = cf.shape
    sq = pl.Squeezed()
    return pl.pallas_call(
        _compress_kernel,
        grid=(two, nb, nkv),
        in_specs=[pl.BlockSpec((sq, sq, sq, nchunk, width), lambda t, b, h: (t, b, h, 0, 0)),
                  pl.BlockSpec((sq, 2 * width, CMP_HIDDEN), lambda t, b, h: (t, 0, 0)),
                  pl.BlockSpec((sq, 1, width), lambda t, b, h: (t, 0, 0)),
                  pl.BlockSpec((sq, 1, width), lambda t, b, h: (t, 0, 0)),
                  pl.BlockSpec((sq, CMP_HIDDEN, 2 * HD), lambda t, b, h: (t, 0, 0)),
                  pl.BlockSpec((sq, HD, CMP_HIDDEN), lambda t, b, h: (t, 0, 0))],
        out_specs=[pl.BlockSpec((sq, sq, sq, nchunk, 2 * HD), lambda t, b, h: (t, b, h, 0, 0)),
                   pl.BlockSpec((sq, sq, sq, HD, nchunk), lambda t, b, h: (t, b, h, 0, 0))],
        out_shape=[jax.ShapeDtypeStruct((two, nb, nkv, nchunk, 2 * HD), _BF),
                   jax.ShapeDtypeStruct((two, nb, nkv, HD, nchunk), _BF)],
        compiler_params=_cparams(("parallel", "parallel", "parallel")),
        name="nsa_compress",
    )(cf, w1, ptop, pbot, w2d, w2t)


def _select_kernel(q_ref, kc_ref, vct_ref, ovt_ref, d0_ref, cb_ref, ocmp_ref, sel_ref, cnt_ref, ot_ref):
    hkv = pl.program_id(1)
    i = pl.program_id(2)
    t0 = (i * TILE).astype(_F32)
    qs = _stack_queries(q_ref, 0, list(range(GQA)))
    nsel = ovt_ref.shape[0]
    nchunk = kc_ref.shape[0] // TILE

    def compressed(rows):
        def run():
            s = lax.dot_general(kc_ref[0:rows, :], qs, _NT, preferred_element_type=_F32) + cb_ref[hkv, 0:rows, :]
            sb = jnp.where(d0_ref[0:rows, :] + t0 >= 0, s, NEG)
            m = jnp.max(sb, axis=0, keepdims=True)
            e = jnp.exp2(sb - m)
            den = jnp.sum(e, axis=0, keepdims=True)
            p = e * jnp.where(m > 0.5 * NEG, 1.0 / den, 0.0)
            o = jnp.dot(vct_ref[:, 0:rows], p.astype(_BF), preferred_element_type=_F32)
            psum = p[:, 0:TILE]
            for j in range(1, GQA):
                psum = psum + p[:, j * TILE:(j + 1) * TILE]
            hi = psum.astype(_BF)
            lo = (psum - hi.astype(_F32)).astype(_BF)
            imp = (jnp.dot(ovt_ref[:, 0:rows], hi, preferred_element_type=_F32)
                   + jnp.dot(ovt_ref[:, 0:rows], lo, preferred_element_type=_F32))
            return o, imp
        return run

    last_block = (i * TILE + TILE - CMP_BLK) // CMP_STRIDE
    o, imp = lax.switch(jnp.minimum(last_block // TILE, nchunk - 1),
                        [compressed((c + 1) * TILE) for c in range(nchunk)])
    for j in range(GQA):
        ot_ref[j * HD:(j + 1) * HD, :] = o[:, j * TILE:(j + 1) * TILE]
    ocmp_ref[...] = ot_ref[...].T.astype(ocmp_ref.dtype)
    jf = lax.broadcasted_iota(jnp.int32, (nsel, TILE), 0).astype(_F32)
    qlane = lax.broadcasted_iota(jnp.int32, (nsel, TILE), 1)
    cur = (i * (TILE // SEL_BLK)).astype(_F32) + jnp.where(qlane >= SEL_BLK, 1.0, 0.0)
    forced = (jf == 0.0) | (jf == cur) | (jf == cur - 1.0)
    free = (jf <= cur) & jnp.logical_not(forced)
    work0 = jnp.where(free, imp, -2.0)
    sel0 = jnp.where(forced, 1.0, 0.0)

    def pick(_, carry):
        work, sel = carry
        mx = jnp.max(work, axis=0, keepdims=True)
        first = jnp.min(jnp.where(work == mx, jf, 1e9), axis=0, keepdims=True)
        hit = (jf == first) & (mx >= 0.0)
        return jnp.where(hit, -2.0, work), jnp.where(hit, 1.0, sel)

    _, sel = lax.fori_loop(0, N_SEL - N_FORCED, pick, (work0, sel0))
    sel_ref[...] = sel
    ones = jnp.ones((8, TILE), _BF)
    cnt_ref[...] = lax.dot_general(ones, sel.astype(_BF), _NT, preferred_element_type=_F32)


def _nsa_select(h1, kc, vct, ovt, d0, cb, nb, seq):
    nq = seq // TILE
    ncmp = kc.shape[2]
    nsel = seq // SEL_BLK
    sq = pl.Squeezed()
    qcb = C_BQ // (2 * LANES)
    return pl.pallas_call(
        _select_kernel,
        grid=(nb, N_KV, nq),
        in_specs=[pl.BlockSpec((TILE, 2 * LANES), lambda b, h, i: (b * nq + i, qcb + h)),
                  pl.BlockSpec((sq, sq, ncmp, 2 * HD), lambda b, h, i: (b, h, 0, 0)),
                  pl.BlockSpec((sq, sq, HD, ncmp), lambda b, h, i: (b, h, 0, 0)),
                  pl.BlockSpec((nsel, ncmp), lambda b, h, i: (0, 0)),
                  pl.BlockSpec((ncmp, GQA * TILE), lambda b, h, i: (0, 0)),
                  pl.BlockSpec((N_KV, ncmp, GQA * TILE), lambda b, h, i: (0, 0, 0))],
        out_specs=[pl.BlockSpec((TILE, 2 * LANES), lambda b, h, i: (b * nq + i, h)),
                   pl.BlockSpec((sq, sq, sq, nsel, TILE), lambda b, h, i: (b, h, i, 0, 0)),
                   pl.BlockSpec((sq, sq, sq, 8, nsel), lambda b, h, i: (b, h, i, 0, 0))],
        out_shape=[jax.ShapeDtypeStruct((nb * seq, 4 * LANES), _BF),
                   jax.ShapeDtypeStruct((nb, N_KV, nq, nsel, TILE), _F32),
                   jax.ShapeDtypeStruct((nb, N_KV, nq, 8, nsel), _F32)],
        scratch_shapes=[pltpu.VMEM((2 * LANES, TILE), _F32)],
        compiler_params=_cparams(("parallel", "parallel", "arbitrary")),
        name="nsa_select",
    )(h1, kc, vct, ovt, d0, cb)


def _selected_kernel(flag_ref, q_ref, k_ref, vt_ref, sel_ref, qfeat_ref, kfeat_ref, dmask_ref, o_ref, ot_ref, list_ref,
                     *, nq):
    b = pl.program_id(0)
    hkv = pl.program_id(1)
    i = pl.program_id(2)
    fbase = ((b * N_KV + hkv) * nq + i) * nq
    n = GQA * TILE
    qa = jnp.concatenate([_stack_queries(q_ref, 0, list(range(GQA))), qfeat_ref[hkv]], axis=1)
    sub = lax.broadcasted_iota(jnp.int32, (8, TILE), 0)

    def scores(jj):
        kt = k_ref[pl.ds(pl.multiple_of(jj * TILE, TILE), TILE), :]
        tiles_back = ((jj - i) * TILE).astype(_F32)
        kf = (kfeat_ref[0] + tiles_back * kfeat_ref[1]).astype(_BF)
        return lax.dot_general(jnp.concatenate([kt, kf], axis=1), qa, _NT, preferred_element_type=_F32)

    def scan(jj, cnt):
        list_ref[cnt] = jj
        return cnt + (flag_ref[fbase + jj] > 0).astype(jnp.int32)

    n_act = lax.fori_loop(0, i, scan, jnp.int32(0))
    for w in range(SEL_WIDE):
        list_ref[n_act + w] = 0

    def sel_row(blk8, r):
        row = jnp.max(jnp.where(sub == r, blk8, 0.0), axis=0, keepdims=True)
        return jnp.concatenate([row] * GQA, axis=1)

    def list_scores(idx):
        jj = list_ref[idx]
        ok = idx < n_act
        blk8 = sel_ref[pl.ds(pl.multiple_of((jj // 4) * 8, 8), 8), :]
        r = jnp.where(ok, (jj % 4) * 2, -2)
        r0 = jnp.where(sel_row(blk8, r) > 0, 0.0, NEG)
        r1 = jnp.where(sel_row(blk8, r + 1) > 0, 0.0, NEG)
        rowbias = jnp.concatenate([jnp.broadcast_to(r0, (SEL_BLK, n)), jnp.broadcast_to(r1, (SEL_BLK, n))], axis=0)
        return scores(jj) + rowbias, jj

    def update(state, tiles):
        m, l, acc = state
        mn = m
        for s, _ in tiles:
            mn = jnp.maximum(mn, jnp.max(s, axis=0, keepdims=True))
        alpha = jnp.exp2(m - mn)
        l = alpha * l
        pv = None
        for s, jj in tiles:
            p = jnp.exp2(s - mn)
            l = l + jnp.sum(p, axis=0, keepdims=True)
            vt = vt_ref[:, pl.ds(pl.multiple_of(jj * TILE, TILE), TILE)]
            part = jnp.dot(vt, p.astype(_BF), preferred_element_type=_F32)
            pv = part if pv is None else pv + part
        return mn, l, alpha * acc + pv

    diag = scores(i) + dmask_ref[...]
    init = (jnp.full((1, n), NEG, _F32), jnp.zeros((1, n), _F32), jnp.zeros((HD, n), _F32))
    state = update(init, [(diag, i)] + [list_scores(w) for w in range(SEL_WIDE - 1)])

    def body(t, st):
        base = SEL_WIDE - 1 + t * SEL_WIDE
        return update(st, [list_scores(base + w) for w in range(SEL_WIDE)])

    ngroups = (jnp.maximum(n_act - (SEL_WIDE - 1), 0) + SEL_WIDE - 1) // SEL_WIDE
    m, l, acc = lax.fori_loop(0, ngroups, body, state)
    o = acc * (1.0 / l)
    for j in range(GQA):
        ot_ref[j * HD:(j + 1) * HD, :] = o[:, j * TILE:(j + 1) * TILE]
    o_ref[...] = ot_ref[...].T.astype(o_ref.dtype)


def _nsa_selected(flags, h1, v1t, sel, qfeat, kfeat, dmask, nb, seq):
    nq = seq // TILE
    nsel = seq // SEL_BLK
    sq = pl.Squeezed()
    qcb = C_BQ // (2 * LANES)
    kcb = C_BSK // LANES
    vrb = R_BSV // HD
    grid_spec = pltpu.PrefetchScalarGridSpec(
        num_scalar_prefetch=1,
        grid=(nb, N_KV, nq),
        in_specs=[pl.BlockSpec((TILE, 2 * LANES), lambda b, h, i, f: (b * nq + i, qcb + h)),
                  pl.BlockSpec((seq, LANES), lambda b, h, i, f: (b, kcb + h)),
                  pl.BlockSpec((sq, HD, seq), lambda b, h, i, f: (0, vrb + h, b)),
                  pl.BlockSpec((sq, sq, sq, nsel, TILE), lambda b, h, i, f: (b, h, i, 0, 0)),
                  pl.BlockSpec((N_KV, GQA * TILE, LANES), lambda b, h, i, f: (0, 0, 0)),
                  pl.BlockSpec((2, TILE, LANES), lambda b, h, i, f: (0, 0, 0)),
                  pl.BlockSpec((TILE, GQA * TILE), lambda b, h, i, f: (0, 0))],
        out_specs=pl.BlockSpec((TILE, 2 * LANES), lambda b, h, i, f: (b * nq + i, h)),
        scratch_shapes=[pltpu.VMEM((2 * LANES, TILE), _F32), pltpu.SMEM((nq + 2 * SEL_WIDE,), jnp.int32)],
    )
    return pl.pallas_call(
        functools.partial(_selected_kernel, nq=nq),
        grid_spec=grid_spec,
        out_shape=jax.ShapeDtypeStruct((nb * seq, 4 * LANES), _BF),
        compiler_params=_cparams(("parallel", "parallel", "arbitrary")),
        name="nsa_selected",
    )(flags, h1, h1, v1t, sel, qfeat, kfeat, dmask)


def _merge_kernel(oa0, oa1, oa2, la0, la1, la2, ocmp, osel, owin, oc, ag, bg, cg, gate, mg0, mg1, mg2,
                  x_ref, wb_ref, wo_ref, lng_ref, lnb_ref, ex_ref, xo_ref, xb_ref, *, alpha):
    f = lambda r: r[...].astype(_F32)
    sig = jax.nn.sigmoid
    silu = lambda v: v * sig(v)
    w = N_HEADS * HD
    sg = sig(f(gate))
    hi = sg.astype(_BF)
    lo = (sg - hi.astype(_F32)).astype(_BF)
    gx = (jnp.dot(hi, ex_ref[...], preferred_element_type=_F32)
          + jnp.dot(lo, ex_ref[...], preferred_element_type=_F32))
    l0, l1, l2 = la0[...], la1[...], la2[...]
    mx = jnp.maximum(jnp.maximum(l0, l1), l2)
    e0, e1, e2 = jnp.exp(l0 - mx), jnp.exp(l1 - mx), jnp.exp(l2 - mx)
    ya = (e0 * f(oa0) + e1 * f(oa1) + e2 * f(oa2)) * (1.0 / (e0 + e1 + e2))
    ya = ya * silu(f(ag))
    yb = gx[:, 0:w] * f(ocmp) + gx[:, w:2 * w] * f(osel) + gx[:, 2 * w:3 * w] * f(owin)
    yb = yb * silu(f(bg))
    yc = f(oc) * silu(f(cg))
    merged = (sig(f(mg0)) * jnp.dot(ya.astype(_BF), wb_ref[0], preferred_element_type=_F32)
              + sig(f(mg1)) * jnp.dot(yb.astype(_BF), wb_ref[1], preferred_element_type=_F32)
              + sig(f(mg2)) * jnp.dot(yc.astype(_BF), wb_ref[2], preferred_element_type=_F32))
    y = jnp.dot(merged.astype(_BF), wo_ref[...], preferred_element_type=_F32)
    z = alpha * x_ref[...] + y
    mu = jnp.mean(z, axis=-1, keepdims=True)
    zc = z - mu
    var = jnp.mean(zc * zc, axis=-1, keepdims=True)
    out = zc * lax.rsqrt(var + LN_EPS) * lng_ref[...] + lnb_ref[...]
    xo_ref[...] = out
    xb_ref[...] = out.astype(_BF)


def _merge(oa, la, ocmp, osel, owin, oc, h1, x, wb, wo, lng, lnb, expand, alpha, tm=256):
    ntok, dm = x.shape
    w = 4 * LANES
    tok = lambda c: pl.BlockSpec((tm, w), lambda i: (i, c))
    h1s = lambda col, width: pl.BlockSpec((tm, width), lambda i: (i, col // width))
    in_specs = ([tok(0)] * 3 + [tok(0)] * 3 + [tok(0)] * 4
                + [h1s(C_AG, w), h1s(C_BG, w), h1s(C_CG, w)]
                + [h1s(C_BGATE, LANES)]
                + [h1s(C_MG + j * dm, dm) for j in range(3)]
                + [pl.BlockSpec((tm, dm), lambda i: (i, 0)),
                   pl.BlockSpec(wb.shape, lambda i: (0, 0, 0)),
                   pl.BlockSpec(wo.shape, lambda i: (0, 0)),
                   pl.BlockSpec((1, dm), lambda i: (0, 0)),
                   pl.BlockSpec((1, dm), lambda i: (0, 0)),
                   pl.BlockSpec(expand.shape, lambda i: (0, 0))])
    return pl.pallas_call(
        functools.partial(_merge_kernel, alpha=alpha),
        grid=(ntok // tm,),
        in_specs=in_specs,
        out_specs=[pl.BlockSpec((tm, dm), lambda i: (i, 0))] * 2,
        out_shape=[jax.ShapeDtypeStruct((ntok, dm), _F32), jax.ShapeDtypeStruct((ntok, dm), _BF)],
        compiler_params=_cparams(("parallel",)),
        name="merge_norm",
    )(*oa, *la, ocmp, osel, owin, oc, *([h1] * 7), x, wb, wo, lng, lnb, expand)


def _alibi(n):
    return 2.0 ** (-8.0 * np.arange(1, n + 1, dtype=np.float64) / n)


def _prep_weights(w_in, b_in, w_cmp1, w_cmp2, cmp_pos, w_branch, w_out):
    widths = (1536, 1536, 1536, 512, 512, 128, 128, 128, 128, 128, 128, 512, 24, 512, 128, 128, 512, 3072)
    names = ("aq", "ak", "av", "ag", "bq", "bck", "bcv", "bsk", "bsv", "bwk", "bwv", "bg", "bgate", "cq", "ck", "cv", "cg", "mg")
    off = dict(zip(names, np.cumsum((0,) + widths[:-1]).tolist()))

    def cols(sl, scale=1.0):
        w, b = w_in[:, :, sl], b_in[:, sl]
        return (w, b) if scale == 1.0 else (w * scale, b * scale)

    def rng(name, start, n):
        return slice(off[name] + start, off[name] + start + n)

    def cat(parts):
        return jnp.concatenate([p[0] for p in parts], axis=2), jnp.concatenate([p[1] for p in parts], axis=1)

    def dup(name):
        return cat([cols(rng(name, kv * HD, HD)) for kv in range(N_KV) for _ in range(2)])

    ngate = 3 * N_HEADS
    pad = H1_COLS - C_BGATE - ngate
    zeros = (jnp.zeros(w_in.shape[:2] + (pad,), w_in.dtype), jnp.zeros(b_in.shape[:1] + (pad,), b_in.dtype))
    tok = [cols(rng("mg", 0, 3072)), cols(rng("aq", 0, 512), Q_SCALE), cols(rng("ak", 0, 512)),
           cols(rng("bq", 0, 512), Q_SCALE), cols(rng("cq", 0, 512), Q_SCALE),
           cols(rng("ag", 0, 512)), cols(rng("bg", 0, 512)), cols(rng("cg", 0, 512)),
           dup("bsk"), dup("bwk"), dup("ck"), cols(rng("bck", 0, 128)), cols(rng("bcv", 0, 128)),
           cols(rng("bgate", 0, ngate)), zeros]
    w1 = jnp.concatenate([t[0].astype(_BF) for t in tok], axis=2)
    b1 = jnp.concatenate([t[1] for t in tok], axis=1)[:, None, :]
    wqk, bqk, wvt, bvt = [], [], [], []
    for gi in (1, 2):
        wq, bq = cols(rng("aq", gi * 512, 512), Q_SCALE)
        wk, bk = cols(rng("ak", gi * 512, 512))
        wqk.append(jnp.concatenate([wq, wk], axis=2).astype(_BF))
        bqk.append(jnp.concatenate([bq, bk], axis=1)[:, None, :])
        wv, bv = cols(rng("av", gi * 512, 512))
        wvt.append(wv.transpose(0, 2, 1).astype(_BF))
        bvt.append(bv[:, :, None])
    feat = [cols(rng("av", 0, 512)), cols(rng("bsv", 0, 128)), cols(rng("bwv", 0, 128)), cols(rng("cv", 0, 128))]
    wv1t = jnp.concatenate([t[0] for t in feat], axis=2).transpose(0, 2, 1).astype(_BF)
    bv1t = jnp.concatenate([t[1] for t in feat], axis=1)[:, :, None]
    half = CMP_STRIDE * HD
    posf = cmp_pos.reshape(cmp_pos.shape[0], 2, 1, CMP_BLK * HD)
    cmpw = dict(w1=w_cmp1.astype(_BF), ptop=posf[..., :half], pbot=posf[..., half:],
                w2d=jnp.concatenate([w_cmp2, w_cmp2], axis=-1).astype(_BF),
                w2t=w_cmp2.transpose(0, 1, 3, 2).astype(_BF))
    return dict(w1=w1, b1=b1, wqk=wqk, bqk=bqk, wvt=wvt, bvt=bvt, wv1t=wv1t, bv1t=bv1t, cmp=cmpw,
                wb=w_branch.astype(_BF), wo=w_out.astype(_BF))


def _constants(seq, sinks):
    a_slopes = _alibi(3 * N_HEADS).reshape(3, N_HEADS)
    bc_slopes = _alibi(N_HEADS)
    no_sink = lambda units, n: jnp.stack([jnp.full((units, n), NEG, _F32), jnp.zeros((units, n), _F32)], axis=1)
    c = {}
    c["a_bias"] = [_banded_bias(a_slopes[gi], win // dil, dil, 2, 2) for gi, (win, dil) in enumerate(A_PATTERNS)]
    c["a_ml"] = no_sink(N_HEADS // 2, 2 * TILE)
    c["bw_bias"] = _banded_bias(bc_slopes, NSA_WINDOW - 1, 1, 1 + NSA_WINDOW // TILE, GQA)
    c["bw_ml"] = no_sink(N_KV, GQA * TILE)
    c["c_bias"] = _banded_bias(bc_slopes, C_WINDOW - 1, 1, 2, GQA)
    sink_rows = jnp.repeat(sinks.astype(_F32).reshape(-1, N_KV, GQA) * LOG2E, TILE, axis=-1)
    c["c_ml"] = jnp.stack([sink_rows, jnp.ones_like(sink_rows)], axis=2)
    lane_slopes = np.repeat(bc_slopes.reshape(N_KV, GQA), TILE, axis=-1)
    ncmp = seq // CMP_STRIDE
    nsel = seq // SEL_BLK
    cs = np.arange(ncmp) * CMP_STRIDE
    ss = np.arange(nsel) * SEL_BLK
    overlap = (cs[None, :] < ss[:, None] + SEL_BLK) & (cs[None, :] + CMP_BLK > ss[:, None]) & (cs[None, :] + CMP_BLK <= seq)
    c["ovt"] = jnp.asarray(overlap, _BF)
    qq = np.tile(np.arange(TILE), GQA)[None, :]
    d0 = qq - (cs[:, None] + CMP_BLK - 1)
    c["d0"] = jnp.asarray(d0, _F32)
    c["cb"] = jnp.asarray(-(lane_slopes * LOG2E)[:, None, :] * d0[None], _F32)
    kk = np.arange(TILE)[:, None]
    c["dmask"] = jnp.asarray(np.where(kk <= qq, 0.0, NEG), _F32)
    rest = lane_slopes * LOG2E
    pieces = []
    for _ in range(3):
        piece = rest.astype(ml_dtypes.bfloat16).astype(np.float64)
        pieces.append(piece)
        rest = rest - piece
    qfeat = np.zeros((N_KV, GQA * TILE, LANES))
    kfeat = np.zeros((2, TILE, LANES))
    for a, piece in enumerate(pieces):
        qfeat[:, :, a] = piece
        qfeat[:, :, 3 + a] = piece
        kfeat[0, :, a] = np.arange(TILE)
        kfeat[1, :, 3 + a] = 1.0
    c["qfeat"] = jnp.asarray(qfeat, _BF)
    c["kfeat"] = jnp.asarray(kfeat, _F32)
    expand = np.zeros((LANES, 3 * N_HEADS * HD))
    for h in range(N_HEADS):
        for j in range(3):
            expand[h * 3 + j, j * N_HEADS * HD + h * HD:j * N_HEADS * HD + (h + 1) * HD] = 1.0
    c["expand"] = jnp.asarray(expand, _BF)
    return c


def kernel(x, w_in, b_in, w_cmp1, w_cmp2, cmp_pos, sinks, w_branch, w_out, ln_g, ln_b):
    nb, seq, dm = x.shape
    depth = w_in.shape[0]
    ntok = nb * seq
    alpha = (2 * depth) ** 0.25
    wts = _prep_weights(w_in, b_in, w_cmp1, w_cmp2, cmp_pos, w_branch, w_out)
    cst = _constants(seq, sinks)
    nchunk = seq // CMP_STRIDE
    nq = seq // TILE
    xf = x.reshape(ntok, dm)
    xb = xf.astype(_BF)
    for l in range(depth):
        h1 = _matmul(xb, wts["w1"][l], wts["b1"][l])
        v1t = _matmul_t(xb, wts["wv1t"][l], wts["bv1t"][l])
        oa, la = [], []
        for gi, (win, dil) in enumerate(A_PATTERNS):
            if dil == 1:
                qk, qcb, kcb, vt, vrb = h1, C_AQ0 // 512, C_AK0 // 512, v1t, R_AV0 // 512
            else:
                xv = xb.reshape(ntok // dil, dil * dm)
                qk = _matmul(xv, wts["wqk"][gi - 1][l], wts["bqk"][gi - 1][l], nclass=dil)
                vt = _matmul_t(xv, wts["wvt"][gi - 1][l], wts["bvt"][gi - 1][l], nclass=dil)
                qcb, kcb, vrb = 0, 1, 0
            o, lse = _banded_attention(qk, qcb, qk, kcb, 4 * LANES, vt, vrb, 4 * LANES, cst["a_bias"][gi], cst["a_ml"],
                                       nb, seq // dil, True, nclass=dil)
            oa.append(o.reshape(ntok, 4 * LANES))
            la.append(lse.reshape(ntok, 4 * LANES))
        cf = h1[:, C_BCK:C_BCK + 2 * LANES].reshape(nb, nchunk, CMP_STRIDE, 2, N_KV, HD)
        cf = cf.transpose(3, 0, 4, 1, 2, 5).reshape(2, nb, N_KV, nchunk, CMP_STRIDE * HD)
        cw = wts["cmp"]
        kc, vct = _compress(cf, cw["w1"][l], cw["ptop"][l], cw["pbot"][l], cw["w2d"][l], cw["w2t"][l])
        ocmp, sel, cnt = _nsa_select(h1, kc[0], vct[1], cst["ovt"], cst["d0"], cst["cb"], nb, seq)
        flags = (cnt[:, :, :, 0, :].reshape(nb, N_KV, nq, nq, 2).sum(-1) > 0).astype(jnp.int32).reshape(-1)
        osel = _nsa_selected(flags, h1, v1t, sel, cst["qfeat"], cst["kfeat"], cst["dmask"], nb, seq)
        owin = _banded_attention(h1, C_BQ // 512, h1, C_BWK // (2 * LANES), 2 * LANES, v1t, R_BWV // LANES, LANES,
                                 cst["bw_bias"], cst["bw_ml"], nb, seq, False)
        oc = _banded_attention(h1, C_CQ // 512, h1, C_CK // (2 * LANES), 2 * LANES, v1t, R_CV // LANES, LANES,
                               cst["c_bias"], cst["c_ml"][l], nb, seq, False)
        xf, xb = _merge(oa, la, ocmp, osel, owin, oc, h1, xf, wts["wb"][l], wts["wo"][l],
                        ln_g[l][None, :], ln_b[l][None, :], cst["expand"], alpha)
    return xf.reshape(nb, seq, dm).astype(x.dtype)
```

```python
import functools
import math

import ml_dtypes
import numpy as np
import jax
import jax.numpy as jnp
from jax import lax
from jax.experimental import pallas as pl
from jax.experimental.pallas import tpu as pltpu

HD = 64
TILE = 128
LANES = 128
N_HEADS = 8
N_KV = 2
GQA = N_HEADS // N_KV
A_PATTERNS = ((128, 1), (512, 4), (2048, 16))
CMP_BLK, CMP_STRIDE, CMP_HIDDEN = 32, 16, 256
SEL_BLK, N_SEL, N_FORCED = 64, 16, 3
NSA_WINDOW, C_WINDOW = 512, 128
SEL_WIDE = 4
LN_EPS = 1e-5
NEG = -1e30
ATTN_SCALE = HD ** -0.5
LOG2E = math.log2(math.e)
LN2 = math.log(2.0)
Q_SCALE = ATTN_SCALE * LOG2E
VMEM_LIMIT = 56 * 1024 * 1024

C_MG, C_AQ0, C_AK0, C_BQ, C_CQ = 0, 3072, 3584, 4096, 4608
C_AG, C_BG, C_CG = 5120, 5632, 6144
C_BSK, C_BWK, C_CK, C_BCK, C_BCV, C_BGATE = 6656, 6912, 7168, 7424, 7552, 7680
H1_COLS = 8192
R_AV0, R_BSV, R_BWV, R_CV, V1_ROWS = 0, 512, 640, 768, 896

_BF = jnp.bfloat16
_F32 = jnp.float32
_NT = (((1,), (1,)), ((), ()))


def _cparams(sem):
    return pltpu.CompilerParams(dimension_semantics=sem, vmem_limit_bytes=VMEM_LIMIT)


def _mm_kernel(x_ref, w_ref, b_ref, o_ref):
    acc = jnp.dot(x_ref[...], w_ref[...], preferred_element_type=_F32)
    o_ref[...] = (acc + b_ref[...]).astype(o_ref.dtype)


def _mm_t_kernel(x_ref, wt_ref, bt_ref, o_ref):
    acc = lax.dot_general(wt_ref[...], x_ref[...], _NT, preferred_element_type=_F32)
    o_ref[...] = (acc + bt_ref[...]).astype(o_ref.dtype)


def _fold_rows(x_refs, xs_ref, r):
    n = xs_ref.shape[0] // r
    for j, x_ref in enumerate(x_refs):
        for c in range(r):
            xs_ref[c * n:(c + 1) * n, j * LANES:(j + 1) * LANES] = x_ref[pl.ds(c, n, stride=r), :].astype(xs_ref.dtype)
    return n


def _mm_fold_kernel(*refs, r):
    (w_ref, b_ref, o_ref, xs_ref), x_refs = refs[-4:], refs[:-4]
    n = _fold_rows(x_refs, xs_ref, r)
    acc = jnp.dot(xs_ref[...], w_ref[...], preferred_element_type=_F32) + b_ref[...]
    for c in range(r):
        o_ref[c] = acc[c * n:(c + 1) * n, :].astype(o_ref.dtype)


def _mm_t_fold_kernel(*refs, r):
    (wt_ref, bt_ref, o_ref, xs_ref), x_refs = refs[-4:], refs[:-4]
    n = _fold_rows(x_refs, xs_ref, r)
    acc = lax.dot_general(wt_ref[...], xs_ref[...], _NT, preferred_element_type=_F32) + bt_ref[...]
    for c in range(r):
        o_ref[c] = acc[:, c * n:(c + 1) * n].astype(o_ref.dtype)


def _matmul(x, w, b, tm=1024, tn=1024):
    m, k = x.shape
    n = w.shape[1]
    tn = min(tn, n)
    return pl.pallas_call(
        _mm_kernel,
        grid=(m // tm, n // tn),
        in_specs=[pl.BlockSpec((tm, k), lambda i, j: (i, 0)),
                  pl.BlockSpec((k, tn), lambda i, j: (0, j)),
                  pl.BlockSpec((1, tn), lambda i, j: (0, j))],
        out_specs=pl.BlockSpec((tm, tn), lambda i, j: (i, j)),
        out_shape=jax.ShapeDtypeStruct((m, n), _BF),
        compiler_params=_cparams(("parallel", "arbitrary")),
        name="proj_tok",
    )(x, w, b)


def _matmul_t(x, wt, bt, tm=1024):
    m, k = x.shape
    n = wt.shape[0]
    return pl.pallas_call(
        _mm_t_kernel,
        grid=(m // tm,),
        in_specs=[pl.BlockSpec((tm, k), lambda i: (i, 0)),
                  pl.BlockSpec((n, k), lambda i: (0, 0)),
                  pl.BlockSpec((n, 1), lambda i: (0, 0))],
        out_specs=pl.BlockSpec((n, tm), lambda i: (0, i)),
        out_shape=jax.ShapeDtypeStruct((n, m), _BF),
        compiler_params=_cparams(("parallel",)),
        name="proj_feat",
    )(x, wt, bt)


def _matmul_fold(x, w, b, r, tm=1024):
    m, k = x.shape
    n = w.shape[1]
    return pl.pallas_call(
        functools.partial(_mm_fold_kernel, r=r),
        grid=(m // tm,),
        in_specs=[pl.BlockSpec((tm, LANES), functools.partial(lambda i, j: (i, j), j=j)) for j in range(k // LANES)]
                 + [pl.BlockSpec((k, n), lambda i: (0, 0)),
                  pl.BlockSpec((1, n), lambda i: (0, 0))],
        out_specs=pl.BlockSpec((r, tm // r, n), lambda i: (0, i, 0)),
        out_shape=jax.ShapeDtypeStruct((r, m // r, n), _BF),
        scratch_shapes=[pltpu.VMEM((tm, k), _BF)],
        compiler_params=_cparams(("parallel",)),
        name="proj_tok_fold",
    )(*([x] * (k // LANES)), w, b)


def _matmul_t_fold(x, wt, bt, r, tm=2048):
    m, k = x.shape
    n = wt.shape[0]
    return pl.pallas_call(
        functools.partial(_mm_t_fold_kernel, r=r),
        grid=(m // tm,),
        in_specs=[pl.BlockSpec((tm, LANES), functools.partial(lambda i, j: (i, j), j=j)) for j in range(k // LANES)]
                 + [pl.BlockSpec((n, k), lambda i: (0, 0)),
                  pl.BlockSpec((n, 1), lambda i: (0, 0))],
        out_specs=pl.BlockSpec((r, n, tm // r), lambda i: (0, 0, i)),
        out_shape=jax.ShapeDtypeStruct((r, n, m // r), _BF),
        scratch_shapes=[pltpu.VMEM((tm, k), _BF)],
        compiler_params=_cparams(("parallel",)),
        name="proj_feat_fold",
    )(*([x] * (k // LANES)), wt, bt)


def _stack_queries(q_ref, col0, heads):
    lane = lax.broadcasted_iota(jnp.int32, (TILE, LANES), 1)
    parts = []
    for h in heads:
        pair, half = divmod(h, 2)
        qp = q_ref[:, col0 + pair * LANES: col0 + (pair + 1) * LANES]
        keep = (lane >= HD) if half else (lane < HD)
        parts.append(jnp.where(keep, qp, jnp.zeros_like(qp)))
    return parts[0] if len(parts) == 1 else jnp.concatenate(parts, axis=0)


def _run_skewed(unit_stages):
    nstage = max(len(s) for s in unit_stages)
    for t in range(len(unit_stages) + nstage - 1):
        for u in range(min(t, len(unit_stages) - 1), -1, -1):
            if t - u < len(unit_stages[u]):
                unit_stages[u][t - u]()


def _banded_kernel(*refs, n_delta, units, heads_per_unit, shared_values, want_lse):
    q_ref = refs[0]
    k_refs = refs[1:1 + n_delta]
    v_refs = refs[1 + n_delta:1 + 2 * n_delta]
    bias_ref, ml_ref = refs[1 + 2 * n_delta:3 + 2 * n_delta]
    outs = refs[3 + 2 * n_delta:]
    o_ref = outs[0]
    lse_ref = outs[1] if want_lse else None
    ot_ref = outs[2] if want_lse else outs[1]
    lt_ref = outs[3] if want_lse else None
    i = pl.program_id(2)
    g = heads_per_unit
    vr = HD if shared_values else g * HD

    def unit(u):
        st = {}
        heads = [u * g + j for j in range(g)]

        def scores():
            qs = _stack_queries(q_ref, 0, heads)
            ss = []
            for d in range(n_delta):
                kt = k_refs[d][:, u * LANES:(u + 1) * LANES]
                s = lax.dot_general(kt, qs, _NT, preferred_element_type=_F32) + bias_ref[u, d]
                if d > 0:
                    s = s + jnp.where(i < d, NEG, 0.0)
                ss.append(s)
            st["s"] = ss

        def softmax():
            ss = st.pop("s")
            m0 = ml_ref[u, 0:1, :]
            m = m0
            for s in ss:
                m = jnp.maximum(m, jnp.max(s, axis=0, keepdims=True))
            l = ml_ref[u, 1:2, :] * jnp.exp2(m0 - m)
            ps = []
            for s in ss:
                p = jnp.exp2(s - m)
                l = l + jnp.sum(p, axis=0, keepdims=True)
                ps.append(p.astype(_BF))
            st["p"], st["m"], st["l"] = ps, m, l

        def values():
            ps, m, l = st.pop("p"), st.pop("m"), st.pop("l")
            acc = None
            for d in range(n_delta):
                part = jnp.dot(v_refs[d][u * vr:(u + 1) * vr, :], ps[d], preferred_element_type=_F32)
                acc = part if acc is None else acc + part
            o = acc * (1.0 / l)
            lse = (m + jnp.log2(l)) * LN2
            for j, h in enumerate(heads):
                r0 = 0 if shared_values else j * HD
                ot_ref[h * HD:(h + 1) * HD, :] = o[r0:r0 + HD, j * TILE:(j + 1) * TILE]
                if want_lse:
                    lt_ref[h * HD:(h + 1) * HD, :] = jnp.broadcast_to(lse[:, j * TILE:(j + 1) * TILE], (HD, TILE))

        return [scores, softmax, values]

    _run_skewed([unit(u) for u in range(units)])
    o_ref[...] = ot_ref[...].T.astype(o_ref.dtype)
    if want_lse:
        lse_ref[...] = lt_ref[...].T


def _banded_bias(slopes, window, dist_scale, n_delta, heads_per_unit):
    kk = np.arange(TILE)[:, None]
    qq = np.arange(TILE)[None, :]
    out = []
    slopes = np.asarray(slopes, np.float64).reshape(-1, heads_per_unit)
    for unit_slopes in slopes:
        per_d = []
        for d in range(n_delta):
            dist = qq - kk + d * TILE
            ok = (dist >= 0) & (dist <= window)
            per_d.append(np.concatenate([np.where(ok, -s * LOG2E * dist_scale * dist, NEG) for s in unit_slopes], axis=1))
        out.append(np.stack(per_d))
    return jnp.asarray(np.stack(out), _F32)


def _banded_attention(q_arr, q_cb, k_arr, k_cb, k_lanes, vt_arr, v_rb, v_rows, bias, ml, nbatch, seq, want_lse,
                      nclass=1):
    nrow = nbatch * seq
    nq = seq // TILE
    units, n_delta = bias.shape[0], bias.shape[1]
    g = N_HEADS // units
    sq = pl.Squeezed()

    def kmap(d):
        return lambda b, c, i: (c, b * nq + jnp.maximum(i - d, 0), k_cb)

    def vmap_(d):
        return lambda b, c, i: (c, v_rb, b * nq + jnp.maximum(i - d, 0))

    in_specs = [pl.BlockSpec((sq, TILE, 4 * LANES), lambda b, c, i: (c, b * nq + i, q_cb))]
    in_specs += [pl.BlockSpec((sq, TILE, k_lanes), kmap(d)) for d in range(n_delta)]
    in_specs += [pl.BlockSpec((sq, v_rows, TILE), vmap_(d)) for d in range(n_delta)]
    in_specs += [pl.BlockSpec(bias.shape, lambda b, c, i: (0, 0, 0, 0)),
                 pl.BlockSpec(ml.shape, lambda b, c, i: (0, 0, 0))]
    out_spec = pl.BlockSpec((sq, TILE, 4 * LANES), lambda b, c, i: (c, b * nq + i, 0))
    out_shape = [jax.ShapeDtypeStruct((nclass, nrow, 4 * LANES), _BF)]
    out_specs = [out_spec]
    scratch = [pltpu.VMEM((4 * LANES, TILE), _F32)]
    if want_lse:
        out_shape.append(jax.ShapeDtypeStruct((nclass, nrow, 4 * LANES), _F32))
        out_specs.append(out_spec)
        scratch.append(pltpu.VMEM((4 * LANES, TILE), _F32))
    kern = functools.partial(_banded_kernel, n_delta=n_delta, units=units, heads_per_unit=g,
                             shared_values=v_rows == units * HD, want_lse=want_lse)
    res = pl.pallas_call(
        kern, grid=(nbatch, nclass, nq), in_specs=in_specs, out_specs=out_specs, out_shape=out_shape,
        scratch_shapes=scratch, compiler_params=_cparams(("parallel", "parallel", "arbitrary")),
        name=f"banded_u{units}_d{n_delta}",
    )(q_arr, *([k_arr] * n_delta), *([vt_arr] * n_delta), bias, ml)
    return res if want_lse else res[0]


def _compress_kernel(cf_ref, w1_ref, ptop_ref, pbot_ref, w2d_ref, w2t_ref, kc_ref, vct_ref):
    nchunk = cf_ref.shape[0]
    half = CMP_STRIDE * HD
    c = cf_ref[...].astype(_F32)
    top = jnp.dot((c + ptop_ref[...]).astype(_BF), w1_ref[0:half, :], preferred_element_type=_F32)
    bot = jnp.dot((c + pbot_ref[...]).astype(_BF), w1_ref[half:2 * half, :], preferred_element_type=_F32)
    hid = top + pltpu.roll(bot, nchunk - 1, 0)
    act = jax.nn.gelu(hid, approximate=True).astype(_BF)
    kc_ref[...] = jnp.dot(act, w2d_ref[...], preferred_element_type=_F32).astype(kc_ref.dtype)
    vct_ref[...] = lax.dot_general(w2t_ref[...], act, _NT, preferred_element_type=_F32).astype(vct_ref.dtype)


def _compress(cf, w1, ptop, pbot, w2d, w2t):
    two, nb, nkv, nchunk, width = cf.shape
    sq = pl.Squeezed()
    return pl.pallas_call(
        _compress_kernel,
        grid=(two, nb, nkv),
        in_specs=[pl.BlockSpec((sq, sq, sq, nchunk, width), lambda t, b, h: (t, b, h, 0, 0)),
                  pl.BlockSpec((sq, 2 * width, CMP_HIDDEN), lambda t, b, h: (t, 0, 0)),
                  pl.BlockSpec((sq, 1, width), lambda t, b, h: (t, 0, 0)),
                  pl.BlockSpec((sq, 1, width), lambda t, b, h: (t, 0, 0)),
                  pl.BlockSpec((sq, CMP_HIDDEN, 2 * HD), lambda t, b, h: (t, 0, 0)),
                  pl.BlockSpec((sq, HD, CMP_HIDDEN), lambda t, b, h: (t, 0, 0))],
        out_specs=[pl.BlockSpec((sq, sq, sq, nchunk, 2 * HD), lambda t, b, h: (t, b, h, 0, 0)),
                   pl.BlockSpec((sq, sq, sq, HD, nchunk), lambda t, b, h: (t, b, h, 0, 0))],
        out_shape=[jax.ShapeDtypeStruct((two, nb, nkv, nchunk, 2 * HD), _BF),
                   jax.ShapeDtypeStruct((two, nb, nkv, HD, nchunk), _BF)],
        compiler_params=_cparams(("parallel", "parallel", "parallel")),
        name="nsa_compress",
    )(cf, w1, ptop, pbot, w2d, w2t)


def _select_kernel(q_ref, kc_ref, vct_ref, ovt_ref, d0_ref, cb_ref, ocmp_ref, sel_ref, cnt_ref, ot_ref):
    hkv = pl.program_id(1)
    i = pl.program_id(2)
    t0 = (i * TILE).astype(_F32)
    qs = _stack_queries(q_ref, 0, list(range(GQA)))
    nsel = ovt_ref.shape[0]
    nchunk = kc_ref.shape[0] // TILE

    def compressed(rows):
        def run():
            s = lax.dot_general(kc_ref[0:rows, :], qs, _NT, preferred_element_type=_F32) + cb_ref[hkv, 0:rows, :]
            sb = jnp.where(d0_ref[0:rows, :] + t0 >= 0, s, NEG)
            m = jnp.max(sb, axis=0, keepdims=True)
            e = jnp.exp2(sb - m)
            den = jnp.sum(e, axis=0, keepdims=True)
            p = e * jnp.where(m > 0.5 * NEG, 1.0 / den, 0.0)
            o = jnp.dot(vct_ref[:, 0:rows], p.astype(_BF), preferred_element_type=_F32)
            psum = p[:, 0:TILE]
            for j in range(1, GQA):
                psum = psum + p[:, j * TILE:(j + 1) * TILE]
            hi = psum.astype(_BF)
            lo = (psum - hi.astype(_F32)).astype(_BF)
            imp = (jnp.dot(ovt_ref[:, 0:rows], hi, preferred_element_type=_F32)
                   + jnp.dot(ovt_ref[:, 0:rows], lo, preferred_element_type=_F32))
            return o, imp
        return run

    last_block = (i * TILE + TILE - CMP_BLK) // CMP_STRIDE
    o, imp = lax.switch(jnp.minimum(last_block // TILE, nchunk - 1),
                        [compressed((c + 1) * TILE) for c in range(nchunk)])
    for j in range(GQA):
        ot_ref[j * HD:(j + 1) * HD, :] = o[:, j * TILE:(j + 1) * TILE]
    ocmp_ref[...] = ot_ref[...].T.astype(ocmp_ref.dtype)
    jf = lax.broadcasted_iota(jnp.int32, (nsel, TILE), 0).astype(_F32)
    qlane = lax.broadcasted_iota(jnp.int32, (nsel, TILE), 1)
    cur = (i * (TILE // SEL_BLK)).astype(_F32) + jnp.where(qlane >= SEL_BLK, 1.0, 0.0)
    forced = (jf == 0.0) | (jf == cur) | (jf == cur - 1.0)
    free = (jf <= cur) & jnp.logical_not(forced)
    work0 = jnp.where(free, imp, -2.0)
    sel0 = jnp.where(forced, 1.0, 0.0)

    def pick(_, carry):
        work, sel = carry
        mx = jnp.max(work, axis=0, keepdims=True)
        first = jnp.min(jnp.where(work == mx, jf, 1e9), axis=0, keepdims=True)
        hit = (jf == first) & (mx >= 0.0)
        return jnp.where(hit, -2.0, work), jnp.where(hit, 1.0, sel)

    _, sel = lax.fori_loop(0, N_SEL - N_FORCED, pick, (work0, sel0))
    sel_ref[...] = sel
    ones = jnp.ones((8, TILE), _BF)
    cnt_ref[...] = lax.dot_general(ones, sel.astype(_BF), _NT, preferred_element_type=_F32)


def _nsa_select(h1, kc, vct, ovt, d0, cb, nb, seq):
    nq = seq // TILE
    ncmp = kc.shape[2]
    nsel = seq // SEL_BLK
    sq = pl.Squeezed()
    qcb = C_BQ // (2 * LANES)
    return pl.pallas_call(
        _select_kernel,
        grid=(nb, N_KV, nq),
        in_specs=[pl.BlockSpec((TILE, 2 * LANES), lambda b, h, i: (b * nq + i, qcb + h)),
                  pl.BlockSpec((sq, sq, ncmp, 2 * HD), lambda b, h, i: (b, h, 0, 0)),
                  pl.BlockSpec((sq, sq, HD, ncmp), lambda b, h, i: (b, h, 0, 0)),
                  pl.BlockSpec((nsel, ncmp), lambda b, h, i: (0, 0)),
                  pl.BlockSpec((ncmp, GQA * TILE), lambda b, h, i: (0, 0)),
                  pl.BlockSpec((N_KV, ncmp, GQA * TILE), lambda b, h, i: (0, 0, 0))],
        out_specs=[pl.BlockSpec((TILE, 2 * LANES), lambda b, h, i: (b * nq + i, h)),
                   pl.BlockSpec((sq, sq, sq, nsel, TILE), lambda b, h, i: (b, h, i, 0, 0)),
                   pl.BlockSpec((sq, sq, sq, 8, nsel), lambda b, h, i: (b, h, i, 0, 0))],
        out_shape=[jax.ShapeDtypeStruct((nb * seq, 4 * LANES), _BF),
                   jax.ShapeDtypeStruct((nb, N_KV, nq, nsel, TILE), _F32),
                   jax.ShapeDtypeStruct((nb, N_KV, nq, 8, nsel), _F32)],
        scratch_shapes=[pltpu.VMEM((2 * LANES, TILE), _F32)],
        compiler_params=_cparams(("parallel", "parallel", "arbitrary")),
        name="nsa_select",
    )(h1, kc, vct, ovt, d0, cb)


def _selected_kernel(*refs, nq, nprob):
    flag_ref, q_ref, k_ref = refs[0:3]
    vt_refs = refs[3:3 + nprob]
    sel_ref, qfeat_ref, kfeat_ref, dmask_ref, o_ref, ot_ref, list_ref = refs[3 + nprob:]
    hkv = pl.program_id(0)
    i = pl.program_id(1)
    n = GQA * TILE
    lstride = nq + 2 * SEL_WIDE
    sub = lax.broadcasted_iota(jnp.int32, (8, TILE), 0)

    def problem(pb):
        fbase = ((pb * N_KV + hkv) * nq + i) * nq
        lbase = pb * lstride
        qa = jnp.concatenate([_stack_queries(q_ref.at[pb], 0, list(range(GQA))), qfeat_ref[hkv]], axis=1)

        def scores(jj):
            kt = k_ref[pb, pl.ds(pl.multiple_of(jj * TILE, TILE), TILE), :]
            tiles_back = ((jj - i) * TILE).astype(_F32)
            kf = (kfeat_ref[0] + tiles_back * kfeat_ref[1]).astype(_BF)
            return lax.dot_general(jnp.concatenate([kt, kf], axis=1), qa, _NT, preferred_element_type=_F32)

        def scan(jj, cnt):
            list_ref[lbase + cnt] = jj
            return cnt + (flag_ref[fbase + jj] > 0).astype(jnp.int32)

        n_act = lax.fori_loop(0, i, scan, jnp.int32(0))
        for w in range(SEL_WIDE):
            list_ref[lbase + n_act + w] = 0

        def sel_row(blk8, r):
            row = jnp.max(jnp.where(sub == r, blk8, 0.0), axis=0, keepdims=True)
            return jnp.concatenate([row] * GQA, axis=1)

        def list_scores(idx):
            jj = list_ref[lbase + idx]
            ok = idx < n_act
            blk8 = sel_ref[pb, pl.ds(pl.multiple_of((jj // 4) * 8, 8), 8), :]
            r = jnp.where(ok, (jj % 4) * 2, -2)
            r0 = jnp.where(sel_row(blk8, r) > 0, 0.0, NEG)
            r1 = jnp.where(sel_row(blk8, r + 1) > 0, 0.0, NEG)
            rowbias = jnp.concatenate([jnp.broadcast_to(r0, (SEL_BLK, n)), jnp.broadcast_to(r1, (SEL_BLK, n))], axis=0)
            return scores(jj) + rowbias, jj

        def update(state, tiles):
            m, l, acc = state
            mn = m
            for s, _ in tiles:
                mn = jnp.maximum(mn, jnp.max(s, axis=0, keepdims=True))
            alpha = jnp.exp2(m - mn)
            l = alpha * l
            pv = None
            for s, jj in tiles:
                p = jnp.exp2(s - mn)
                l = l + jnp.sum(p, axis=0, keepdims=True)
                vt = vt_refs[pb][:, pl.ds(pl.multiple_of(jj * TILE, TILE), TILE)]
                part = jnp.dot(vt, p.astype(_BF), preferred_element_type=_F32)
                pv = part if pv is None else pv + part
            return mn, l, alpha * acc + pv

        def first(state):
            diag = scores(i) + dmask_ref[...]
            return update(state, [(diag, i)] + [list_scores(w) for w in range(SEL_WIDE - 1)])

        def group(t, state):
            base = SEL_WIDE - 1 + t * SEL_WIDE
            return update(state, [list_scores(base + w) for w in range(SEL_WIDE)])

        ngroups = (jnp.maximum(n_act - (SEL_WIDE - 1), 0) + SEL_WIDE - 1) // SEL_WIDE
        return first, group, ngroups

    probs = [problem(pb) for pb in range(nprob)]
    init = (jnp.full((1, n), NEG, _F32), jnp.zeros((1, n), _F32), jnp.zeros((HD, n), _F32))
    states = tuple(first(init) for first, _, _ in probs)
    ngroups = probs[0][2]
    for _, _, ng in probs[1:]:
        ngroups = jnp.maximum(ngroups, ng)
    states = lax.fori_loop(0, ngroups, lambda t, sts: tuple(group(t, st) for (_, group, _), st in zip(probs, sts)),
                           states)
    for pb, (m, l, acc) in enumerate(states):
        o = acc * (1.0 / l)
        for j in range(GQA):
            ot_ref[j * HD:(j + 1) * HD, :] = o[:, j * TILE:(j + 1) * TILE]
        o_ref[pb] = ot_ref[...].T.astype(o_ref.dtype)


def _nsa_selected(flags, h1, v1t, sel, qfeat, kfeat, dmask, nb, seq):
    nq = seq // TILE
    nsel = seq // SEL_BLK
    sq = pl.Squeezed()
    qcb = C_BQ // (2 * LANES)
    kcb = C_BSK // LANES
    vrb = R_BSV // HD
    h3 = h1.reshape(nb, seq, h1.shape[-1])
    grid_spec = pltpu.PrefetchScalarGridSpec(
        num_scalar_prefetch=1,
        grid=(N_KV, nq),
        in_specs=[pl.BlockSpec((nb, TILE, 2 * LANES), lambda h, i, f: (0, i, qcb + h)),
                  pl.BlockSpec((nb, seq, LANES), lambda h, i, f: (0, 0, kcb + h))]
                 + [pl.BlockSpec((sq, HD, seq), functools.partial(lambda h, i, f, b: (0, vrb + h, b), b=b))
                    for b in range(nb)]
                 + [pl.BlockSpec((nb, sq, sq, nsel, TILE), lambda h, i, f: (0, h, i, 0, 0)),
                    pl.BlockSpec((N_KV, GQA * TILE, LANES), lambda h, i, f: (0, 0, 0)),
                    pl.BlockSpec((2, TILE, LANES), lambda h, i, f: (0, 0, 0)),
                    pl.BlockSpec((TILE, GQA * TILE), lambda h, i, f: (0, 0))],
        out_specs=pl.BlockSpec((nb, TILE, 2 * LANES), lambda h, i, f: (0, i, h)),
        scratch_shapes=[pltpu.VMEM((2 * LANES, TILE), _F32), pltpu.SMEM((nb * (nq + 2 * SEL_WIDE),), jnp.int32)],
    )
    return pl.pallas_call(
        functools.partial(_selected_kernel, nq=nq, nprob=nb),
        grid_spec=grid_spec,
        out_shape=jax.ShapeDtypeStruct((nb, seq, 4 * LANES), _BF),
        compiler_params=_cparams(("parallel", "arbitrary")),
        name="nsa_selected",
    )(flags, h3, h3, *([v1t] * nb), sel, qfeat, kfeat, dmask)


def _merge_kernel(oa0, oa1, oa2, la0, la1, la2, ocmp, osel, owin, oc, ag, bg, cg, gate, mg0, mg1, mg2,
                  x_ref, wb_ref, wo_ref, lng_ref, lnb_ref, ex_ref, xo_ref, xb_ref, *u_refs, alpha):
    f = lambda r: r[...].astype(_F32)
    sig = jax.nn.sigmoid
    silu = lambda v: v * sig(v)
    w = N_HEADS * HD
    sg = sig(f(gate))
    hi = sg.astype(_BF)
    lo = (sg - hi.astype(_F32)).astype(_BF)
    gx = (jnp.dot(hi, ex_ref[...], preferred_element_type=_F32)
          + jnp.dot(lo, ex_ref[...], preferred_element_type=_F32))
    def unfold(src, dst):
        r, n = src.shape[0], src.shape[1]
        for c in range(r):
            blk = src[c].astype(_F32)
            for j in range(dst.shape[0]):
                dst[j, pl.ds(c, n, stride=r), :] = blk[:, j * LANES:(j + 1) * LANES]
        return jnp.concatenate([dst[j] for j in range(dst.shape[0])], axis=1)

    o1, l1, o2, l2 = (unfold(src, dst) for src, dst in zip((oa1, la1, oa2, la2), u_refs))
    l0 = la0[...]
    mx = jnp.maximum(jnp.maximum(l0, l1), l2)
    e0, e1, e2 = jnp.exp(l0 - mx), jnp.exp(l1 - mx), jnp.exp(l2 - mx)
    ya = (e0 * f(oa0) + e1 * o1 + e2 * o2) * (1.0 / (e0 + e1 + e2))
    ya = ya * silu(f(ag))
    yb = gx[:, 0:w] * f(ocmp) + gx[:, w:2 * w] * f(osel) + gx[:, 2 * w:3 * w] * f(owin)
    yb = yb * silu(f(bg))
    yc = f(oc) * silu(f(cg))
    merged = (sig(f(mg0)) * jnp.dot(ya.astype(_BF), wb_ref[0], preferred_element_type=_F32)
              + sig(f(mg1)) * jnp.dot(yb.astype(_BF), wb_ref[1], preferred_element_type=_F32)
              + sig(f(mg2)) * jnp.dot(yc.astype(_BF), wb_ref[2], preferred_element_type=_F32))
    y = jnp.dot(merged.astype(_BF), wo_ref[...], preferred_element_type=_F32)
    z = alpha * x_ref[...] + y
    mu = jnp.mean(z, axis=-1, keepdims=True)
    zc = z - mu
    var = jnp.mean(zc * zc, axis=-1, keepdims=True)
    out = zc * lax.rsqrt(var + LN_EPS) * lng_ref[...] + lnb_ref[...]
    xo_ref[...] = out
    xb_ref[...] = out.astype(_BF)


def _merge(oa, la, ocmp, osel, owin, oc, h1, x, wb, wo, lng, lnb, expand, alpha, tm=256):
    ntok, dm = x.shape
    w = 4 * LANES
    tok = lambda c: pl.BlockSpec((tm, w), lambda i: (i, c))
    h1s = lambda col, width: pl.BlockSpec((tm, width), lambda i: (i, col // width))
    cls = lambda a: pl.BlockSpec((a.shape[0], tm // a.shape[0], w), lambda i: (0, i, 0))
    in_specs = ([tok(0), cls(oa[1]), cls(oa[2]), tok(0), cls(la[1]), cls(la[2])] + [tok(0)] * 4
                + [h1s(C_AG, w), h1s(C_BG, w), h1s(C_CG, w)]
                + [h1s(C_BGATE, LANES)]
                + [h1s(C_MG + j * dm, dm) for j in range(3)]
                + [pl.BlockSpec((tm, dm), lambda i: (i, 0)),
                   pl.BlockSpec(wb.shape, lambda i: (0, 0, 0)),
                   pl.BlockSpec(wo.shape, lambda i: (0, 0)),
                   pl.BlockSpec((1, dm), lambda i: (0, 0)),
                   pl.BlockSpec((1, dm), lambda i: (0, 0)),
                   pl.BlockSpec(expand.shape, lambda i: (0, 0))])
    return pl.pallas_call(
        functools.partial(_merge_kernel, alpha=alpha),
        grid=(ntok // tm,),
        in_specs=in_specs,
        out_specs=[pl.BlockSpec((tm, dm), lambda i: (i, 0))] * 2,
        out_shape=[jax.ShapeDtypeStruct((ntok, dm), _F32), jax.ShapeDtypeStruct((ntok, dm), _BF)],
        scratch_shapes=[pltpu.VMEM((w // LANES, tm, LANES), _F32)] * 4,
        compiler_params=_cparams(("parallel",)),
        name="merge_norm",
    )(*oa, *la, ocmp, osel, owin, oc, *([h1] * 7), x, wb, wo, lng, lnb, expand)


def _alibi(n):
    return 2.0 ** (-8.0 * np.arange(1, n + 1, dtype=np.float64) / n)


def _prep_weights(w_in, b_in, w_cmp1, w_cmp2, cmp_pos, w_branch, w_out):
    widths = (1536, 1536, 1536, 512, 512, 128, 128, 128, 128, 128, 128, 512, 24, 512, 128, 128, 512, 3072)
    names = ("aq", "ak", "av", "ag", "bq", "bck", "bcv", "bsk", "bsv", "bwk", "bwv", "bg", "bgate", "cq", "ck", "cv", "cg", "mg")
    off = dict(zip(names, np.cumsum((0,) + widths[:-1]).tolist()))

    def cols(sl, scale=1.0):
        w, b = w_in[:, :, sl], b_in[:, sl]
        return (w, b) if scale == 1.0 else (w * scale, b * scale)

    def rng(name, start, n):
        return slice(off[name] + start, off[name] + start + n)

    def cat(parts):
        return jnp.concatenate([p[0] for p in parts], axis=2), jnp.concatenate([p[1] for p in parts], axis=1)

    def dup(name):
        return cat([cols(rng(name, kv * HD, HD)) for kv in range(N_KV) for _ in range(2)])

    ngate = 3 * N_HEADS
    pad = H1_COLS - C_BGATE - ngate
    zeros = (jnp.zeros(w_in.shape[:2] + (pad,), w_in.dtype), jnp.zeros(b_in.shape[:1] + (pad,), b_in.dtype))
    tok = [cols(rng("mg", 0, 3072)), cols(rng("aq", 0, 512), Q_SCALE), cols(rng("ak", 0, 512)),
           cols(rng("bq", 0, 512), Q_SCALE), cols(rng("cq", 0, 512), Q_SCALE),
           cols(rng("ag", 0, 512)), cols(rng("bg", 0, 512)), cols(rng("cg", 0, 512)),
           dup("bsk"), dup("bwk"), dup("ck"), cols(rng("bck", 0, 128)), cols(rng("bcv", 0, 128)),
           cols(rng("bgate", 0, ngate)), zeros]
    w1 = jnp.concatenate([t[0].astype(_BF) for t in tok], axis=2)
    b1 = jnp.concatenate([t[1] for t in tok], axis=1)[:, None, :]
    wqk, bqk, wvt, bvt = [], [], [], []
    for gi in (1, 2):
        wq, bq = cols(rng("aq", gi * 512, 512), Q_SCALE)
        wk, bk = cols(rng("ak", gi * 512, 512))
        wqk.append(jnp.concatenate([wq, wk], axis=2).astype(_BF))
        bqk.append(jnp.concatenate([bq, bk], axis=1)[:, None, :])
        wv, bv = cols(rng("av", gi * 512, 512))
        wvt.append(wv.transpose(0, 2, 1).astype(_BF))
        bvt.append(bv[:, :, None])
    feat = [cols(rng("av", 0, 512)), cols(rng("bsv", 0, 128)), cols(rng("bwv", 0, 128)), cols(rng("cv", 0, 128))]
    wv1t = jnp.concatenate([t[0] for t in feat], axis=2).transpose(0, 2, 1).astype(_BF)
    bv1t = jnp.concatenate([t[1] for t in feat], axis=1)[:, :, None]
    half = CMP_STRIDE * HD
    posf = cmp_pos.reshape(cmp_pos.shape[0], 2, 1, CMP_BLK * HD)
    cmpw = dict(w1=w_cmp1.astype(_BF), ptop=posf[..., :half], pbot=posf[..., half:],
                w2d=jnp.concatenate([w_cmp2, w_cmp2], axis=-1).astype(_BF),
                w2t=w_cmp2.transpose(0, 1, 3, 2).astype(_BF))
    return dict(w1=w1, b1=b1, wqk=wqk, bqk=bqk, wvt=wvt, bvt=bvt, wv1t=wv1t, bv1t=bv1t, cmp=cmpw,
                wb=w_branch.astype(_BF), wo=w_out.astype(_BF))


def _constants(seq, sinks):
    a_slopes = _alibi(3 * N_HEADS).reshape(3, N_HEADS)
    bc_slopes = _alibi(N_HEADS)
    no_sink = lambda units, n: jnp.stack([jnp.full((units, n), NEG, _F32), jnp.zeros((units, n), _F32)], axis=1)
    c = {}
    c["a_bias"] = [_banded_bias(a_slopes[gi], win // dil, dil, 2, 2) for gi, (win, dil) in enumerate(A_PATTERNS)]
    c["a_ml"] = no_sink(N_HEADS // 2, 2 * TILE)
    c["bw_bias"] = _banded_bias(bc_slopes, NSA_WINDOW - 1, 1, 1 + NSA_WINDOW // TILE, GQA)
    c["bw_ml"] = no_sink(N_KV, GQA * TILE)
    c["c_bias"] = _banded_bias(bc_slopes, C_WINDOW - 1, 1, 2, GQA)
    sink_rows = jnp.repeat(sinks.astype(_F32).reshape(-1, N_KV, GQA) * LOG2E, TILE, axis=-1)
    c["c_ml"] = jnp.stack([sink_rows, jnp.ones_like(sink_rows)], axis=2)
    lane_slopes = np.repeat(bc_slopes.reshape(N_KV, GQA), TILE, axis=-1)
    ncmp = seq // CMP_STRIDE
    nsel = seq // SEL_BLK
    cs = np.arange(ncmp) * CMP_STRIDE
    ss = np.arange(nsel) * SEL_BLK
    overlap = (cs[None, :] < ss[:, None] + SEL_BLK) & (cs[None, :] + CMP_BLK > ss[:, None]) & (cs[None, :] + CMP_BLK <= seq)
    c["ovt"] = jnp.asarray(overlap, _BF)
    qq = np.tile(np.arange(TILE), GQA)[None, :]
    d0 = qq - (cs[:, None] + CMP_BLK - 1)
    c["d0"] = jnp.asarray(d0, _F32)
    c["cb"] = jnp.asarray(-(lane_slopes * LOG2E)[:, None, :] * d0[None], _F32)
    kk = np.arange(TILE)[:, None]
    c["dmask"] = jnp.asarray(np.where(kk <= qq, 0.0, NEG), _F32)
    rest = lane_slopes * LOG2E
    pieces = []
    for _ in range(3):
        piece = rest.astype(ml_dtypes.bfloat16).astype(np.float64)
        pieces.append(piece)
        rest = rest - piece
    qfeat = np.zeros((N_KV, GQA * TILE, LANES))
    kfeat = np.zeros((2, TILE, LANES))
    for a, piece in enumerate(pieces):
        qfeat[:, :, a] = piece
        qfeat[:, :, 3 + a] = piece
        kfeat[0, :, a] = np.arange(TILE)
        kfeat[1, :, 3 + a] = 1.0
    c["qfeat"] = jnp.asarray(qfeat, _BF)
    c["kfeat"] = jnp.asarray(kfeat, _F32)
    expand = np.zeros((LANES, 3 * N_HEADS * HD))
    for h in range(N_HEADS):
        for j in range(3):
            expand[h * 3 + j, j * N_HEADS * HD + h * HD:j * N_HEADS * HD + (h + 1) * HD] = 1.0
    c["expand"] = jnp.asarray(expand, _BF)
    return c


def kernel(x, w_in, b_in, w_cmp1, w_cmp2, cmp_pos, sinks, w_branch, w_out, ln_g, ln_b):
    nb, seq, dm = x.shape
    depth = w_in.shape[0]
    ntok = nb * seq
    alpha = (2 * depth) ** 0.25
    wts = _prep_weights(w_in, b_in, w_cmp1, w_cmp2, cmp_pos, w_branch, w_out)
    cst = _constants(seq, sinks)
    nchunk = seq // CMP_STRIDE
    nq = seq // TILE
    xf = x.reshape(ntok, dm)
    xb = xf.astype(_BF)
    for l in range(depth):
        h1 = _matmul(xb, wts["w1"][l], wts["b1"][l])
        v1t = _matmul_t(xb, wts["wv1t"][l], wts["bv1t"][l])[None]
        h3 = h1[None]
        oa, la = [], []
        for gi, (win, dil) in enumerate(A_PATTERNS):
            if dil == 1:
                qk, qcb, kcb, vt = h3, C_AQ0 // 512, C_AK0 // 512, v1t
            else:
                qk = _matmul_fold(xf, wts["wqk"][gi - 1][l], wts["bqk"][gi - 1][l], dil)
                vt = _matmul_t_fold(xf, wts["wvt"][gi - 1][l], wts["bvt"][gi - 1][l], dil)
                qcb, kcb = 0, 1
            o, lse = _banded_attention(qk, qcb, qk, kcb, 4 * LANES, vt, R_AV0 // 512, 4 * LANES, cst["a_bias"][gi],
                                       cst["a_ml"], nb, seq // dil, True, nclass=dil)
            oa.append(o[0] if dil == 1 else o)
            la.append(lse[0] if dil == 1 else lse)
        cf = h1[:, C_BCK:C_BCK + 2 * LANES].reshape(nb, nchunk, CMP_STRIDE, 2, N_KV, HD)
        cf = cf.transpose(3, 0, 4, 1, 2, 5).reshape(2, nb, N_KV, nchunk, CMP_STRIDE * HD)
        cw = wts["cmp"]
        kc, vct = _compress(cf, cw["w1"][l], cw["ptop"][l], cw["pbot"][l], cw["w2d"][l], cw["w2t"][l])
        ocmp, sel, cnt = _nsa_select(h1, kc[0], vct[1], cst["ovt"], cst["d0"], cst["cb"], nb, seq)
        flags = (cnt[:, :, :, 0, :].reshape(nb, N_KV, nq, nq, 2).sum(-1) > 0).astype(jnp.int32).reshape(-1)
        osel = _nsa_selected(flags, h1, v1t, sel, cst["qfeat"], cst["kfeat"], cst["dmask"], nb, seq).reshape(ntok, 4 * LANES)
        owin = _banded_attention(h3, C_BQ // 512, h3, C_BWK // (2 * LANES), 2 * LANES, v1t, R_BWV // LANES, LANES,
                                 cst["bw_bias"], cst["bw_ml"], nb, seq, False)[0]
        oc = _banded_attention(h3, C_CQ // 512, h3, C_CK // (2 * LANES), 2 * LANES, v1t, R_CV // LANES, LANES,
                               cst["c_bias"], cst["c_ml"][l], nb, seq, False)[0]
        xf, xb = _merge(oa, la, ocmp, osel, owin, oc, h1, xf, wts["wb"][l], wts["wo"][l],
                        ln_g[l][None, :], ln_b[l][None, :], cst["expand"], alpha)
    return xf.reshape(nb, seq, dm).astype(x.dtype)
```

```python
import functools
import math

import ml_dtypes
import numpy as np
import jax
import jax.numpy as jnp
from jax import lax
from jax.experimental import pallas as pl
from jax.experimental.pallas import tpu as pltpu

HD = 64
TILE = 128
LANES = 128
N_HEADS = 8
N_KV = 2
GQA = N_HEADS // N_KV
A_PATTERNS = ((128, 1), (512, 4), (2048, 16))
CMP_BLK, CMP_STRIDE, CMP_HIDDEN = 32, 16, 256
SEL_BLK, N_SEL, N_FORCED = 64, 16, 3
NSA_WINDOW, C_WINDOW = 512, 128
Q_TILES = 4
ONES_ROWS = 16
SEL_WIDE = 4
LN_EPS = 1e-5
NEG = -1e30
ATTN_SCALE = HD ** -0.5
LOG2E = math.log2(math.e)
LN2 = math.log(2.0)
Q_SCALE = ATTN_SCALE * LOG2E
VMEM_LIMIT = 56 * 1024 * 1024

C_MG, C_AQ0, C_AK0, C_BQ, C_CQ = 0, 3072, 3584, 4096, 4608
C_AG, C_BG, C_CG = 5120, 5632, 6144
C_BSK, C_BWK, C_CK, C_BGATE = 6656, 6912, 7168, 7424
H1_COLS = 8192
C16_CMP = 1024
R_AV0, R_BSV, R_BWV, R_CV, V1_ROWS = 0, 512, 640, 768, 896

_BF = jnp.bfloat16
_F32 = jnp.float32
_NT = (((1,), (1,)), ((), ()))


def _cparams(sem):
    return pltpu.CompilerParams(dimension_semantics=sem, vmem_limit_bytes=VMEM_LIMIT)


def _mm_kernel(x_ref, w_ref, b_ref, o_ref):
    acc = jnp.dot(x_ref[...], w_ref[...], preferred_element_type=_F32)
    o_ref[...] = (acc + b_ref[...]).astype(o_ref.dtype)


def _mm_t_kernel(x_ref, wt_ref, bt_ref, o_ref):
    acc = lax.dot_general(wt_ref[...], x_ref[...], _NT, preferred_element_type=_F32)
    o_ref[...] = (acc + bt_ref[...]).astype(o_ref.dtype)


def _fold_rows(x_refs, xs_ref, r):
    n = xs_ref.shape[0] // r
    for j, x_ref in enumerate(x_refs):
        for c in range(r):
            xs_ref[c * n:(c + 1) * n, j * LANES:(j + 1) * LANES] = x_ref[pl.ds(c, n, stride=r), :].astype(xs_ref.dtype)
    return n


def _mm_fold_kernel(*refs, r):
    (w_ref, b_ref, o_ref, xs_ref), x_refs = refs[-4:], refs[:-4]
    n = _fold_rows(x_refs, xs_ref, r)
    acc = jnp.dot(xs_ref[...], w_ref[...], preferred_element_type=_F32) + b_ref[...]
    for c in range(r):
        o_ref[c] = acc[c * n:(c + 1) * n, :].astype(o_ref.dtype)


def _mm_t_fold_kernel(*refs, r):
    (wt_ref, bt_ref, o_ref, xs_ref), x_refs = refs[-4:], refs[:-4]
    n = _fold_rows(x_refs, xs_ref, r)
    acc = lax.dot_general(wt_ref[...], xs_ref[...], _NT, preferred_element_type=_F32) + bt_ref[...]
    for c in range(r):
        o_ref[c] = acc[:, c * n:(c + 1) * n].astype(o_ref.dtype)


def _matmul(x, w, b, tm=1024, tn=1024):
    m, k = x.shape
    n = w.shape[1]
    tn = min(tn, n)
    return pl.pallas_call(
        _mm_kernel,
        grid=(m // tm, n // tn),
        in_specs=[pl.BlockSpec((tm, k), lambda i, j: (i, 0)),
                  pl.BlockSpec((k, tn), lambda i, j: (0, j)),
                  pl.BlockSpec((1, tn), lambda i, j: (0, j))],
        out_specs=pl.BlockSpec((tm, tn), lambda i, j: (i, j)),
        out_shape=jax.ShapeDtypeStruct((m, n), _BF),
        compiler_params=_cparams(("parallel", "arbitrary")),
        name="proj_tok",
    )(x, w, b)


def _matmul_t(x, wt, bt, tm=1024):
    m, k = x.shape
    n = wt.shape[0]
    return pl.pallas_call(
        _mm_t_kernel,
        grid=(m // tm,),
        in_specs=[pl.BlockSpec((tm, k), lambda i: (i, 0)),
                  pl.BlockSpec((n, k), lambda i: (0, 0)),
                  pl.BlockSpec((n, 1), lambda i: (0, 0))],
        out_specs=pl.BlockSpec((n, tm), lambda i: (0, i)),
        out_shape=jax.ShapeDtypeStruct((n, m), _BF),
        compiler_params=_cparams(("parallel",)),
        name="proj_feat",
    )(x, wt, bt)


def _matmul_fold(x, w, b, r, tm=1024):
    m, k = x.shape
    n = w.shape[1]
    return pl.pallas_call(
        functools.partial(_mm_fold_kernel, r=r),
        grid=(m // tm,),
        in_specs=[pl.BlockSpec((tm, LANES), functools.partial(lambda i, j: (i, j), j=j)) for j in range(k // LANES)]
                 + [pl.BlockSpec((k, n), lambda i: (0, 0)),
                  pl.BlockSpec((1, n), lambda i: (0, 0))],
        out_specs=pl.BlockSpec((r, tm // r, n), lambda i: (0, i, 0)),
        out_shape=jax.ShapeDtypeStruct((r, m // r, n), _BF),
        scratch_shapes=[pltpu.VMEM((tm, k), _BF)],
        compiler_params=_cparams(("parallel",)),
        name="proj_tok_fold",
    )(*([x] * (k // LANES)), w, b)


def _matmul_t_fold(x, wt, bt, r, tm=2048):
    m, k = x.shape
    n = wt.shape[0]
    return pl.pallas_call(
        functools.partial(_mm_t_fold_kernel, r=r),
        grid=(m // tm,),
        in_specs=[pl.BlockSpec((tm, LANES), functools.partial(lambda i, j: (i, j), j=j)) for j in range(k // LANES)]
                 + [pl.BlockSpec((n, k), lambda i: (0, 0)),
                  pl.BlockSpec((n, 1), lambda i: (0, 0))],
        out_specs=pl.BlockSpec((r, n, tm // r), lambda i: (0, 0, i)),
        out_shape=jax.ShapeDtypeStruct((r, n, m // r), _BF),
        scratch_shapes=[pltpu.VMEM((tm, k), _BF)],
        compiler_params=_cparams(("parallel",)),
        name="proj_feat_fold",
    )(*([x] * (k // LANES)), wt, bt)


def _stack_queries(q_ref, row0, heads):
    lane = lax.broadcasted_iota(jnp.int32, (TILE, LANES), 1)
    parts = []
    for h in heads:
        pair, half = divmod(h, 2)
        qp = q_ref[row0:row0 + TILE, pair * LANES:(pair + 1) * LANES]
        keep = (lane >= HD) if half else (lane < HD)
        parts.append(jnp.where(keep, qp, jnp.zeros_like(qp)))
    return parts[0] if len(parts) == 1 else jnp.concatenate(parts, axis=0)


def _probs(s, m):
    return jnp.exp2((s - m).astype(_BF))


def _with_ones(vt):
    return jnp.concatenate([vt, jnp.ones((ONES_ROWS, vt.shape[1]), vt.dtype)], axis=0)


def _run_skewed(unit_stages):
    nstage = max(len(s) for s in unit_stages)
    for t in range(len(unit_stages) + nstage - 1):
        for u in range(min(t, len(unit_stages) - 1), -1, -1):
            if t - u < len(unit_stages[u]):
                unit_stages[u][t - u]()


def _banded_kernel(*refs, n_delta, n_blk, units, heads_per_unit, shared_values, want_lse):
    q_ref = refs[0]
    k_refs = refs[1:1 + n_blk]
    v_refs = refs[1 + n_blk:1 + 2 * n_blk]
    bias_ref, ml_ref = refs[1 + 2 * n_blk:3 + 2 * n_blk]
    outs = refs[3 + 2 * n_blk:]
    o_ref = outs[0]
    lse_ref = outs[1] if want_lse else None
    ot_ref = outs[2] if want_lse else outs[1]
    lt_ref = outs[3] if want_lse else None
    step = pl.program_id(2)
    g = heads_per_unit
    vr = HD if shared_values else g * HD

    def unit(a, u):
        st = {}
        heads = [u * g + j for j in range(g)]

        def key_tile(d):
            blocks_back, half = divmod(d - a + Q_TILES - 1, Q_TILES)
            return blocks_back, (Q_TILES - 1 - half) * TILE

        def scores():
            qs = _stack_queries(q_ref, a * TILE, heads)
            ss = []
            for d in range(n_delta):
                blk, r0 = key_tile(d)
                kt = k_refs[blk][r0:r0 + TILE, u * LANES:(u + 1) * LANES]
                s = lax.dot_general(kt, qs, _NT, preferred_element_type=_F32) + bias_ref[u, d]
                if d > a:
                    s = s + jnp.where(step * Q_TILES + a < d, NEG, 0.0)
                ss.append(s)
            st["s"] = ss

        def softmax():
            ss = st.pop("s")
            m0 = ml_ref[u, 0:1, :]
            m = m0
            for s in ss:
                m = jnp.maximum(m, jnp.max(s, axis=0, keepdims=True))
            st["p"], st["m"], st["l0"] = [_probs(s, m) for s in ss], m, ml_ref[u, 1:2, :] * jnp.exp2(m0 - m)

        def values():
            ps, m, l0 = st.pop("p"), st.pop("m"), st.pop("l0")
            acc = None
            for d in range(n_delta):
                blk, r0 = key_tile(d)
                vt = _with_ones(v_refs[blk][u * vr:(u + 1) * vr, r0:r0 + TILE])
                part = jnp.dot(vt, ps[d], preferred_element_type=_F32)
                acc = part if acc is None else acc + part
            l = l0 + acc[vr:vr + 1, :]
            o = acc[0:vr, :] * (1.0 / l)
            lse = (m + jnp.log2(l)) * LN2
            for j, h in enumerate(heads):
                r0 = 0 if shared_values else j * HD
                ot_ref[h * HD:(h + 1) * HD, a * TILE:(a + 1) * TILE] = o[r0:r0 + HD, j * TILE:(j + 1) * TILE]
                if want_lse:
                    lt_ref[h * HD:(h + 1) * HD, a * TILE:(a + 1) * TILE] = jnp.broadcast_to(
                        lse[:, j * TILE:(j + 1) * TILE], (HD, TILE))

        return [scores, softmax, values]

    _run_skewed([unit(a, u) for a in range(Q_TILES) for u in range(units)])
    o_ref[...] = ot_ref[...].T.astype(o_ref.dtype)
    if want_lse:
        lse_ref[...] = lt_ref[...].T


def _banded_bias(slopes, window, dist_scale, n_delta, heads_per_unit):
    kk = np.arange(TILE)[:, None]
    qq = np.arange(TILE)[None, :]
    out = []
    slopes = np.asarray(slopes, np.float64).reshape(-1, heads_per_unit)
    for unit_slopes in slopes:
        per_d = []
        for d in range(n_delta):
            dist = qq - kk + d * TILE
            ok = (dist >= 0) & (dist <= window)
            per_d.append(np.concatenate([np.where(ok, -s * LOG2E * dist_scale * dist, NEG) for s in unit_slopes], axis=1))
        out.append(np.stack(per_d))
    return jnp.asarray(np.stack(out), _F32)


def _banded_attention(q_arr, q_cb, k_arr, k_cb, k_lanes, vt_arr, v_rb, v_rows, bias, ml, nbatch, seq, want_lse,
                      nclass=1):
    nrow = nbatch * seq
    rows = Q_TILES * TILE
    nq = seq // rows
    units, n_delta = bias.shape[0], bias.shape[1]
    n_blk = 1 + -(-(n_delta - 1) // Q_TILES)
    g = N_HEADS // units
    sq = pl.Squeezed()

    def kmap(o):
        return lambda b, c, i: (c, b * nq + jnp.maximum(i - o, 0), k_cb)

    def vmap_(o):
        return lambda b, c, i: (c, v_rb, b * nq + jnp.maximum(i - o, 0))

    in_specs = [pl.BlockSpec((sq, rows, 4 * LANES), lambda b, c, i: (c, b * nq + i, q_cb))]
    in_specs += [pl.BlockSpec((sq, rows, k_lanes), kmap(o)) for o in range(n_blk)]
    in_specs += [pl.BlockSpec((sq, v_rows, rows), vmap_(o)) for o in range(n_blk)]
    in_specs += [pl.BlockSpec(bias.shape, lambda b, c, i: (0, 0, 0, 0)),
                 pl.BlockSpec(ml.shape, lambda b, c, i: (0, 0, 0))]
    out_spec = pl.BlockSpec((sq, rows, 4 * LANES), lambda b, c, i: (c, b * nq + i, 0))
    out_shape = [jax.ShapeDtypeStruct((nclass, nrow, 4 * LANES), _BF)]
    out_specs = [out_spec]
    scratch = [pltpu.VMEM((4 * LANES, rows), _F32)]
    if want_lse:
        out_shape.append(jax.ShapeDtypeStruct((nclass, nrow, 4 * LANES), _F32))
        out_specs.append(out_spec)
        scratch.append(pltpu.VMEM((4 * LANES, rows), _F32))
    kern = functools.partial(_banded_kernel, n_delta=n_delta, n_blk=n_blk, units=units, heads_per_unit=g,
                             shared_values=v_rows == units * HD, want_lse=want_lse)
    res = pl.pallas_call(
        kern, grid=(nbatch, nclass, nq), in_specs=in_specs, out_specs=out_specs, out_shape=out_shape,
        scratch_shapes=scratch, compiler_params=_cparams(("parallel", "parallel", "arbitrary")),
        name=f"banded_u{units}_d{n_delta}",
    )(q_arr, *([k_arr] * n_blk), *([vt_arr] * n_blk), bias, ml)
    return res if want_lse else res[0]


def _compress_kernel(x_ref, w1t_ref, w1b_ref, ptop_ref, pbot_ref, w2d_ref, w2t_ref, kc_ref, vct_ref):
    nchunk = x_ref.shape[1]
    c = jnp.concatenate([x_ref[p] for p in range(CMP_STRIDE)], axis=1).astype(_F32)
    top = jnp.dot((c + ptop_ref[...]).astype(_BF), w1t_ref[...], preferred_element_type=_F32)
    bot = jnp.dot((c + pbot_ref[...]).astype(_BF), w1b_ref[...], preferred_element_type=_F32)
    hid = top + pltpu.roll(bot, nchunk - 1, 0)
    act = jax.nn.gelu(hid, approximate=True).astype(_BF)
    kc_ref[...] = jnp.dot(act, w2d_ref[...], preferred_element_type=_F32).astype(kc_ref.dtype)
    vct_ref[...] = lax.dot_general(w2t_ref[...], act, _NT, preferred_element_type=_F32).astype(vct_ref.dtype)


def _compress(x16, col0, w1t, w1b, ptop, pbot, w2d, w2t, nb, nchunk):
    two, nkv = 2, N_KV
    width = CMP_STRIDE * LANES
    sq = pl.Squeezed()
    return pl.pallas_call(
        _compress_kernel,
        grid=(two, nb, nkv),
        in_specs=[pl.BlockSpec((CMP_STRIDE, nchunk, LANES), lambda t, b, h: (0, b, col0 + t * nkv + h)),
                  pl.BlockSpec((sq, width, CMP_HIDDEN), lambda t, b, h: (t, 0, 0)),
                  pl.BlockSpec((sq, width, CMP_HIDDEN), lambda t, b, h: (t, 0, 0)),
                  pl.BlockSpec((sq, 1, width), lambda t, b, h: (t, 0, 0)),
                  pl.BlockSpec((sq, 1, width), lambda t, b, h: (t, 0, 0)),
                  pl.BlockSpec((sq, CMP_HIDDEN, 2 * HD), lambda t, b, h: (t, 0, 0)),
                  pl.BlockSpec((sq, HD, CMP_HIDDEN), lambda t, b, h: (t, 0, 0))],
        out_specs=[pl.BlockSpec((sq, sq, sq, nchunk, 2 * HD), lambda t, b, h: (t, b, h, 0, 0)),
                   pl.BlockSpec((sq, sq, sq, HD, nchunk), lambda t, b, h: (t, b, h, 0, 0))],
        out_shape=[jax.ShapeDtypeStruct((two, nb, nkv, nchunk, 2 * HD), _BF),
                   jax.ShapeDtypeStruct((two, nb, nkv, HD, nchunk), _BF)],
        compiler_params=_cparams(("parallel", "parallel", "parallel")),
        name="nsa_compress",
    )(x16, w1t, w1b, ptop, pbot, w2d, w2t)


def _select_kernel(q_ref, kc_ref, vct_ref, ovt_ref, d0_ref, cb_ref, ocmp_ref, sel_ref, cnt_ref, ot_ref):
    hkv = pl.program_id(1)
    i = pl.program_id(2)
    t0 = (i * TILE).astype(_F32)
    qs = _stack_queries(q_ref, 0, list(range(GQA)))
    nsel = ovt_ref.shape[0]
    nchunk = kc_ref.shape[0] // TILE

    def compressed(rows):
        def run():
            s = lax.dot_general(kc_ref[0:rows, :], qs, _NT, preferred_element_type=_F32) + cb_ref[hkv, 0:rows, :]
            sb = jnp.where(d0_ref[0:rows, :] + t0 >= 0, s, NEG)
            m = jnp.max(sb, axis=0, keepdims=True)
            e = jnp.exp2(sb - m)
            den = jnp.sum(e, axis=0, keepdims=True)
            p = e * jnp.where(m > 0.5 * NEG, 1.0 / den, 0.0)
            o = jnp.dot(vct_ref[:, 0:rows], p.astype(_BF), preferred_element_type=_F32)
            psum = p[:, 0:TILE]
            for j in range(1, GQA):
                psum = psum + p[:, j * TILE:(j + 1) * TILE]
            hi = psum.astype(_BF)
            lo = (psum - hi.astype(_F32)).astype(_BF)
            imp = (jnp.dot(ovt_ref[:, 0:rows], hi, preferred_element_type=_F32)
                   + jnp.dot(ovt_ref[:, 0:rows], lo, preferred_element_type=_F32))
            return o, imp
        return run

    last_block = (i * TILE + TILE - CMP_BLK) // CMP_STRIDE
    o, imp = lax.switch(jnp.minimum(last_block // TILE, nchunk - 1),
                        [compressed((c + 1) * TILE) for c in range(nchunk)])
    for j in range(GQA):
        ot_ref[j * HD:(j + 1) * HD, :] = o[:, j * TILE:(j + 1) * TILE]
    ocmp_ref[...] = ot_ref[...].T.astype(ocmp_ref.dtype)
    jf = lax.broadcasted_iota(jnp.int32, (nsel, TILE), 0).astype(_F32)
    qlane = lax.broadcasted_iota(jnp.int32, (nsel, TILE), 1)
    cur = (i * (TILE // SEL_BLK)).astype(_F32) + jnp.where(qlane >= SEL_BLK, 1.0, 0.0)
    forced = (jf == 0.0) | (jf == cur) | (jf == cur - 1.0)
    free = (jf <= cur) & jnp.logical_not(forced)
    work0 = jnp.where(free, imp, -2.0)
    sel0 = jnp.where(forced, 1.0, 0.0)

    def pick(_, carry):
        work, sel = carry
        mx = jnp.max(work, axis=0, keepdims=True)
        first = jnp.min(jnp.where(work == mx, jf, 1e9), axis=0, keepdims=True)
        hit = (jf == first) & (mx >= 0.0)
        return jnp.where(hit, -2.0, work), jnp.where(hit, 1.0, sel)

    _, sel = lax.fori_loop(0, N_SEL - N_FORCED, pick, (work0, sel0))
    sel_ref[...] = sel
    ones = jnp.ones((8, TILE), _BF)
    cnt_ref[...] = lax.dot_general(ones, sel.astype(_BF), _NT, preferred_element_type=_F32)


def _nsa_select(h1, kc, vct, ovt, d0, cb, nb, seq):
    nq = seq // TILE
    ncmp = kc.shape[2]
    nsel = seq // SEL_BLK
    sq = pl.Squeezed()
    qcb = C_BQ // (2 * LANES)
    return pl.pallas_call(
        _select_kernel,
        grid=(nb, N_KV, nq),
        in_specs=[pl.BlockSpec((TILE, 2 * LANES), lambda b, h, i: (b * nq + i, qcb + h)),
                  pl.BlockSpec((sq, sq, ncmp, 2 * HD), lambda b, h, i: (b, h, 0, 0)),
                  pl.BlockSpec((sq, sq, HD, ncmp), lambda b, h, i: (b, h, 0, 0)),
                  pl.BlockSpec((nsel, ncmp), lambda b, h, i: (0, 0)),
                  pl.BlockSpec((ncmp, GQA * TILE), lambda b, h, i: (0, 0)),
                  pl.BlockSpec((N_KV, ncmp, GQA * TILE), lambda b, h, i: (0, 0, 0))],
        out_specs=[pl.BlockSpec((TILE, 2 * LANES), lambda b, h, i: (b * nq + i, h)),
                   pl.BlockSpec((sq, sq, sq, nsel, TILE), lambda b, h, i: (b, h, i, 0, 0)),
                   pl.BlockSpec((sq, sq, sq, 8, nsel), lambda b, h, i: (b, h, i, 0, 0))],
        out_shape=[jax.ShapeDtypeStruct((nb * seq, 4 * LANES), _BF),
                   jax.ShapeDtypeStruct((nb, N_KV, nq, nsel, TILE), _F32),
                   jax.ShapeDtypeStruct((nb, N_KV, nq, 8, nsel), _F32)],
        scratch_shapes=[pltpu.VMEM((2 * LANES, TILE), _F32)],
        compiler_params=_cparams(("parallel", "parallel", "arbitrary")),
        name="nsa_select",
    )(h1, kc, vct, ovt, d0, cb)


def _selected_kernel(*refs, nq, nprob):
    flag_ref, q_ref, k_ref = refs[0:3]
    vt_refs = refs[3:3 + nprob]
    sel_ref, qfeat_ref, kfeat_ref, dmask_ref, o_ref, ot_ref, list_ref = refs[3 + nprob:]
    hkv = pl.program_id(0)
    i = pl.program_id(1)
    n = GQA * TILE
    lstride = nq + 2 * SEL_WIDE
    sub = lax.broadcasted_iota(jnp.int32, (8, TILE), 0)

    def problem(pb):
        fbase = ((pb * N_KV + hkv) * nq + i) * nq
        lbase = pb * lstride
        qa = jnp.concatenate([_stack_queries(q_ref.at[pb], 0, list(range(GQA))), qfeat_ref[hkv]], axis=1)

        def scores(jj):
            kt = k_ref[pb, pl.ds(pl.multiple_of(jj * TILE, TILE), TILE), :]
            tiles_back = ((jj - i) * TILE).astype(_F32)
            kf = (kfeat_ref[0] + tiles_back * kfeat_ref[1]).astype(_BF)
            return lax.dot_general(jnp.concatenate([kt, kf], axis=1), qa, _NT, preferred_element_type=_F32)

        def scan(jj, cnt):
            list_ref[lbase + cnt] = jj
            return cnt + (flag_ref[fbase + jj] > 0).astype(jnp.int32)

        n_act = lax.fori_loop(0, i, scan, jnp.int32(0))
        for w in range(SEL_WIDE):
            list_ref[lbase + n_act + w] = 0

        def sel_row(blk8, r):
            row = jnp.max(jnp.where(sub == r, blk8, 0.0), axis=0, keepdims=True)
            return jnp.concatenate([row] * GQA, axis=1)

        def list_scores(idx):
            jj = list_ref[lbase + idx]
            ok = idx < n_act
            blk8 = sel_ref[pb, pl.ds(pl.multiple_of((jj // 4) * 8, 8), 8), :]
            r = jnp.where(ok, (jj % 4) * 2, -2)
            r0 = jnp.where(sel_row(blk8, r) > 0, 0.0, NEG)
            r1 = jnp.where(sel_row(blk8, r + 1) > 0, 0.0, NEG)
            rowbias = jnp.concatenate([jnp.broadcast_to(r0, (SEL_BLK, n)), jnp.broadcast_to(r1, (SEL_BLK, n))], axis=0)
            return scores(jj) + rowbias, jj

        def update(state, tiles):
            m, acc = state
            mn = m
            for s, _ in tiles:
                mn = jnp.maximum(mn, jnp.max(s, axis=0, keepdims=True))
            alpha = jnp.exp2(m - mn)
            pv = None
            for s, jj in tiles:
                vt = _with_ones(vt_refs[pb][:, pl.ds(pl.multiple_of(jj * TILE, TILE), TILE)])
                part = jnp.dot(vt, _probs(s, mn), preferred_element_type=_F32)
                pv = part if pv is None else pv + part
            return mn, alpha * acc + pv

        def first(state):
            diag = scores(i) + dmask_ref[...]
            return update(state, [(diag, i)] + [list_scores(w) for w in range(SEL_WIDE - 1)])

        def group(t, state):
            base = SEL_WIDE - 1 + t * SEL_WIDE
            return update(state, [list_scores(base + w) for w in range(SEL_WIDE)])

        ngroups = (jnp.maximum(n_act - (SEL_WIDE - 1), 0) + SEL_WIDE - 1) // SEL_WIDE
        return first, group, ngroups

    probs = [problem(pb) for pb in range(nprob)]
    init = (jnp.full((1, n), NEG, _F32), jnp.zeros((HD + ONES_ROWS, n), _F32))
    states = tuple(first(init) for first, _, _ in probs)
    ngroups = probs[0][2]
    for _, _, ng in probs[1:]:
        ngroups = jnp.maximum(ngroups, ng)
    states = lax.fori_loop(0, ngroups, lambda t, sts: tuple(group(t, st) for (_, group, _), st in zip(probs, sts)),
                           states)
    for pb, (m, acc) in enumerate(states):
        o = acc[0:HD, :] * (1.0 / acc[HD:HD + 1, :])
        for j in range(GQA):
            ot_ref[j * HD:(j + 1) * HD, :] = o[:, j * TILE:(j + 1) * TILE]
        o_ref[pb] = ot_ref[...].T.astype(o_ref.dtype)


def _nsa_selected(flags, h1, v1t, sel, qfeat, kfeat, dmask, nb, seq):
    nq = seq // TILE
    nsel = seq // SEL_BLK
    sq = pl.Squeezed()
    qcb = C_BQ // (2 * LANES)
    kcb = C_BSK // LANES
    vrb = R_BSV // HD
    h3 = h1.reshape(nb, seq, h1.shape[-1])
    grid_spec = pltpu.PrefetchScalarGridSpec(
        num_scalar_prefetch=1,
        grid=(N_KV, nq),
        in_specs=[pl.BlockSpec((nb, TILE, 2 * LANES), lambda h, i, f: (0, i, qcb + h)),
                  pl.BlockSpec((nb, seq, LANES), lambda h, i, f: (0, 0, kcb + h))]
                 + [pl.BlockSpec((sq, HD, seq), functools.partial(lambda h, i, f, b: (0, vrb + h, b), b=b))
                    for b in range(nb)]
                 + [pl.BlockSpec((nb, sq, sq, nsel, TILE), lambda h, i, f: (0, h, i, 0, 0)),
                    pl.BlockSpec((N_KV, GQA * TILE, LANES), lambda h, i, f: (0, 0, 0)),
                    pl.BlockSpec((2, TILE, LANES), lambda h, i, f: (0, 0, 0)),
                    pl.BlockSpec((TILE, GQA * TILE), lambda h, i, f: (0, 0))],
        out_specs=pl.BlockSpec((nb, TILE, 2 * LANES), lambda h, i, f: (0, i, h)),
        scratch_shapes=[pltpu.VMEM((2 * LANES, TILE), _F32), pltpu.SMEM((nb * (nq + 2 * SEL_WIDE),), jnp.int32)],
    )
    return pl.pallas_call(
        functools.partial(_selected_kernel, nq=nq, nprob=nb),
        grid_spec=grid_spec,
        out_shape=jax.ShapeDtypeStruct((nb, seq, 4 * LANES), _BF),
        compiler_params=_cparams(("parallel", "arbitrary")),
        name="nsa_selected",
    )(flags, h3, h3, *([v1t] * nb), sel, qfeat, kfeat, dmask)


def _merge_kernel(oa0, oa1, oa2, la0, la1, la2, ocmp, osel, owin, oc, ag, bg, cg, gate, mg0, mg1, mg2,
                  x_ref, wb_ref, wo_ref, lng_ref, lnb_ref, ex_ref, xo_ref, xb_ref, *u_refs, alpha):
    f = lambda r: r[...].astype(_F32)
    sig = lambda v: 0.5 * jnp.tanh(0.5 * v) + 0.5
    silu = lambda v: v * sig(v)
    w = N_HEADS * HD
    sg = sig(f(gate))
    hi = sg.astype(_BF)
    lo = (sg - hi.astype(_F32)).astype(_BF)
    gx = (jnp.dot(hi, ex_ref[...], preferred_element_type=_F32)
          + jnp.dot(lo, ex_ref[...], preferred_element_type=_F32))
    def unfold(src, dst):
        r, n = src.shape[0], src.shape[1]
        for c in range(r):
            blk = src[c].astype(_F32)
            for j in range(dst.shape[0]):
                dst[j, pl.ds(c, n, stride=r), :] = blk[:, j * LANES:(j + 1) * LANES]
        return jnp.concatenate([dst[j] for j in range(dst.shape[0])], axis=1)

    o1, l1, o2, l2 = (unfold(src, dst) for src, dst in zip((oa1, la1, oa2, la2), u_refs))
    l0 = la0[...]
    mx = jnp.maximum(jnp.maximum(l0, l1), l2)
    e0, e1, e2 = jnp.exp(l0 - mx), jnp.exp(l1 - mx), jnp.exp(l2 - mx)
    ya = (e0 * f(oa0) + e1 * o1 + e2 * o2) * (1.0 / (e0 + e1 + e2))
    ya = ya * silu(f(ag))
    yb = gx[:, 0:w] * f(ocmp) + gx[:, w:2 * w] * f(osel) + gx[:, 2 * w:3 * w] * f(owin)
    yb = yb * silu(f(bg))
    yc = f(oc) * silu(f(cg))
    merged = (sig(f(mg0)) * jnp.dot(ya.astype(_BF), wb_ref[0], preferred_element_type=_F32)
              + sig(f(mg1)) * jnp.dot(yb.astype(_BF), wb_ref[1], preferred_element_type=_F32)
              + sig(f(mg2)) * jnp.dot(yc.astype(_BF), wb_ref[2], preferred_element_type=_F32))
    y = jnp.dot(merged.astype(_BF), wo_ref[...], preferred_element_type=_F32)
    z = alpha * x_ref[...] + y
    mu = jnp.mean(z, axis=-1, keepdims=True)
    zc = z - mu
    var = jnp.mean(zc * zc, axis=-1, keepdims=True)
    out = zc * lax.rsqrt(var + LN_EPS) * lng_ref[...] + lnb_ref[...]
    xo_ref[...] = out
    xb_ref[...] = out.astype(_BF)


def _merge(oa, la, ocmp, osel, owin, oc, h1, x, wb, wo, lng, lnb, expand, alpha, tm=256):
    ntok, dm = x.shape
    w = 4 * LANES
    tok = lambda c: pl.BlockSpec((tm, w), lambda i: (i, c))
    h1s = lambda col, width: pl.BlockSpec((tm, width), lambda i: (i, col // width))
    cls = lambda a: pl.BlockSpec((a.shape[0], tm // a.shape[0], w), lambda i: (0, i, 0))
    in_specs = ([tok(0), cls(oa[1]), cls(oa[2]), tok(0), cls(la[1]), cls(la[2])] + [tok(0)] * 4
                + [h1s(C_AG, w), h1s(C_BG, w), h1s(C_CG, w)]
                + [h1s(C_BGATE, LANES)]
                + [h1s(C_MG + j * dm, dm) for j in range(3)]
                + [pl.BlockSpec((tm, dm), lambda i: (i, 0)),
                   pl.BlockSpec(wb.shape, lambda i: (0, 0, 0)),
                   pl.BlockSpec(wo.shape, lambda i: (0, 0)),
                   pl.BlockSpec((1, dm), lambda i: (0, 0)),
                   pl.BlockSpec((1, dm), lambda i: (0, 0)),
                   pl.BlockSpec(expand.shape, lambda i: (0, 0))])
    return pl.pallas_call(
        functools.partial(_merge_kernel, alpha=alpha),
        grid=(ntok // tm,),
        in_specs=in_specs,
        out_specs=[pl.BlockSpec((tm, dm), lambda i: (i, 0))] * 2,
        out_shape=[jax.ShapeDtypeStruct((ntok, dm), _F32), jax.ShapeDtypeStruct((ntok, dm), _BF)],
        scratch_shapes=[pltpu.VMEM((w // LANES, tm, LANES), _F32)] * 4,
        compiler_params=_cparams(("parallel",)),
        name="merge_norm",
    )(*oa, *la, ocmp, osel, owin, oc, *([h1] * 7), x, wb, wo, lng, lnb, expand)


def _alibi(n):
    return 2.0 ** (-8.0 * np.arange(1, n + 1, dtype=np.float64) / n)


def _prep_weights(w_in, b_in, w_cmp1, w_cmp2, cmp_pos, w_branch, w_out):
    widths = (1536, 1536, 1536, 512, 512, 128, 128, 128, 128, 128, 128, 512, 24, 512, 128, 128, 512, 3072)
    names = ("aq", "ak", "av", "ag", "bq", "bck", "bcv", "bsk", "bsv", "bwk", "bwv", "bg", "bgate", "cq", "ck", "cv", "cg", "mg")
    off = dict(zip(names, np.cumsum((0,) + widths[:-1]).tolist()))

    def cols(sl, scale=1.0):
        w, b = w_in[:, :, sl], b_in[:, sl]
        return (w, b) if scale == 1.0 else (w * scale, b * scale)

    def rng(name, start, n):
        return slice(off[name] + start, off[name] + start + n)

    def cat(parts):
        return jnp.concatenate([p[0] for p in parts], axis=2), jnp.concatenate([p[1] for p in parts], axis=1)

    def dup(name):
        return cat([cols(rng(name, kv * HD, HD)) for kv in range(N_KV) for _ in range(2)])

    ngate = 3 * N_HEADS
    pad = H1_COLS - C_BGATE - ngate
    zeros = (jnp.zeros(w_in.shape[:2] + (pad,), w_in.dtype), jnp.zeros(b_in.shape[:1] + (pad,), b_in.dtype))
    tok = [cols(rng("mg", 0, 3072)), cols(rng("aq", 0, 512), Q_SCALE), cols(rng("ak", 0, 512)),
           cols(rng("bq", 0, 512), Q_SCALE), cols(rng("cq", 0, 512), Q_SCALE),
           cols(rng("ag", 0, 512)), cols(rng("bg", 0, 512)), cols(rng("cg", 0, 512)),
           dup("bsk"), dup("bwk"), dup("ck"),
           cols(rng("bgate", 0, ngate)), zeros]
    w1 = jnp.concatenate([t[0].astype(_BF) for t in tok], axis=2)
    b1 = jnp.concatenate([t[1] for t in tok], axis=1)[:, None, :]
    wqk, bqk, wvt, bvt = [], [], [], []
    for gi in (1, 2):
        wq, bq = cols(rng("aq", gi * 512, 512), Q_SCALE)
        wk, bk = cols(rng("ak", gi * 512, 512))
        extra = []
        if A_PATTERNS[gi][1] == CMP_STRIDE:
            zw, zb = jnp.zeros(w_in.shape[:2] + (HD,), w_in.dtype), jnp.zeros(b_in.shape[:1] + (HD,), b_in.dtype)
            for name in ("bck", "bcv"):
                for kv in range(N_KV):
                    extra += [cols(rng(name, kv * HD, HD)), (zw, zb)]
        wqk.append(jnp.concatenate([wq, wk] + [e[0] for e in extra], axis=2).astype(_BF))
        bqk.append(jnp.concatenate([bq, bk] + [e[1] for e in extra], axis=1)[:, None, :])
        wv, bv = cols(rng("av", gi * 512, 512))
        wvt.append(wv.transpose(0, 2, 1).astype(_BF))
        bvt.append(bv[:, :, None])
    feat = [cols(rng("av", 0, 512)), cols(rng("bsv", 0, 128)), cols(rng("bwv", 0, 128)), cols(rng("cv", 0, 128))]
    wv1t = jnp.concatenate([t[0] for t in feat], axis=2).transpose(0, 2, 1).astype(_BF)
    bv1t = jnp.concatenate([t[1] for t in feat], axis=1)[:, :, None]
    depth = w_cmp1.shape[0]
    w1p = jnp.pad(w_cmp1.reshape(depth, 2, CMP_BLK, HD, CMP_HIDDEN), ((0, 0),) * 3 + ((0, LANES - HD), (0, 0)))
    w1p = w1p.reshape(depth, 2, 2, CMP_STRIDE * LANES, CMP_HIDDEN).astype(_BF)
    posp = jnp.pad(cmp_pos, ((0, 0),) * 3 + ((0, LANES - HD),)).reshape(depth, 2, 2, 1, CMP_STRIDE * LANES)
    cmpw = dict(w1t=w1p[:, :, 0], w1b=w1p[:, :, 1], ptop=posp[:, :, 0], pbot=posp[:, :, 1],
                w2d=jnp.concatenate([w_cmp2, w_cmp2], axis=-1).astype(_BF),
                w2t=w_cmp2.transpose(0, 1, 3, 2).astype(_BF))
    return dict(w1=w1, b1=b1, wqk=wqk, bqk=bqk, wvt=wvt, bvt=bvt, wv1t=wv1t, bv1t=bv1t, cmp=cmpw,
                wb=w_branch.astype(_BF), wo=w_out.astype(_BF))


def _constants(seq, sinks):
    a_slopes = _alibi(3 * N_HEADS).reshape(3, N_HEADS)
    bc_slopes = _alibi(N_HEADS)
    no_sink = lambda units, n: jnp.stack([jnp.full((units, n), NEG, _F32), jnp.zeros((units, n), _F32)], axis=1)
    c = {}
    c["a_bias"] = [_banded_bias(a_slopes[gi], win // dil, dil, 2, 2) for gi, (win, dil) in enumerate(A_PATTERNS)]
    c["a_ml"] = no_sink(N_HEADS // 2, 2 * TILE)
    c["bw_bias"] = _banded_bias(bc_slopes, NSA_WINDOW - 1, 1, 1 + NSA_WINDOW // TILE, GQA)
    c["bw_ml"] = no_sink(N_KV, GQA * TILE)
    c["c_bias"] = _banded_bias(bc_slopes, C_WINDOW - 1, 1, 2, GQA)
    sink_rows = jnp.repeat(sinks.astype(_F32).reshape(-1, N_KV, GQA) * LOG2E, TILE, axis=-1)
    c["c_ml"] = jnp.stack([sink_rows, jnp.ones_like(sink_rows)], axis=2)
    lane_slopes = np.repeat(bc_slopes.reshape(N_KV, GQA), TILE, axis=-1)
    ncmp = seq // CMP_STRIDE
    nsel = seq // SEL_BLK
    cs = np.arange(ncmp) * CMP_STRIDE
    ss = np.arange(nsel) * SEL_BLK
    overlap = (cs[None, :] < ss[:, None] + SEL_BLK) & (cs[None, :] + CMP_BLK > ss[:, None]) & (cs[None, :] + CMP_BLK <= seq)
    c["ovt"] = jnp.asarray(overlap, _BF)
    qq = np.tile(np.arange(TILE), GQA)[None, :]
    d0 = qq - (cs[:, None] + CMP_BLK - 1)
    c["d0"] = jnp.asarray(d0, _F32)
    c["cb"] = jnp.asarray(-(lane_slopes * LOG2E)[:, None, :] * d0[None], _F32)
    kk = np.arange(TILE)[:, None]
    c["dmask"] = jnp.asarray(np.where(kk <= qq, 0.0, NEG), _F32)
    rest = lane_slopes * LOG2E
    pieces = []
    for _ in range(3):
        piece = rest.astype(ml_dtypes.bfloat16).astype(np.float64)
        pieces.append(piece)
        rest = rest - piece
    qfeat = np.zeros((N_KV, GQA * TILE, LANES))
    kfeat = np.zeros((2, TILE, LANES))
    for a, piece in enumerate(pieces):
        qfeat[:, :, a] = piece
        qfeat[:, :, 3 + a] = piece
        kfeat[0, :, a] = np.arange(TILE)
        kfeat[1, :, 3 + a] = 1.0
    c["qfeat"] = jnp.asarray(qfeat, _BF)
    c["kfeat"] = jnp.asarray(kfeat, _F32)
    expand = np.zeros((LANES, 3 * N_HEADS * HD))
    for h in range(N_HEADS):
        for j in range(3):
            expand[h * 3 + j, j * N_HEADS * HD + h * HD:j * N_HEADS * HD + (h + 1) * HD] = 1.0
    c["expand"] = jnp.asarray(expand, _BF)
    return c


def kernel(x, w_in, b_in, w_cmp1, w_cmp2, cmp_pos, sinks, w_branch, w_out, ln_g, ln_b):
    nb, seq, dm = x.shape
    depth = w_in.shape[0]
    ntok = nb * seq
    alpha = (2 * depth) ** 0.25
    wts = _prep_weights(w_in, b_in, w_cmp1, w_cmp2, cmp_pos, w_branch, w_out)
    cst = _constants(seq, sinks)
    nchunk = seq // CMP_STRIDE
    nq = seq // TILE
    xf = x.reshape(ntok, dm)
    xb = xf.astype(_BF)
    for l in range(depth):
        h1 = _matmul(xb, wts["w1"][l], wts["b1"][l])
        v1t = _matmul_t(xb, wts["wv1t"][l], wts["bv1t"][l])[None]
        h3 = h1[None]
        oa, la = [], []
        for gi, (win, dil) in enumerate(A_PATTERNS):
            if dil == 1:
                qk, qcb, kcb, vt = h3, C_AQ0 // 512, C_AK0 // 512, v1t
            else:
                qk = _matmul_fold(xf, wts["wqk"][gi - 1][l], wts["bqk"][gi - 1][l], dil)
                vt = _matmul_t_fold(xf, wts["wvt"][gi - 1][l], wts["bvt"][gi - 1][l], dil)
                qcb, kcb = 0, 1
                if dil == CMP_STRIDE:
                    x16 = qk
            o, lse = _banded_attention(qk, qcb, qk, kcb, 4 * LANES, vt, R_AV0 // 512, 4 * LANES, cst["a_bias"][gi],
                                       cst["a_ml"], nb, seq // dil, True, nclass=dil)
            oa.append(o[0] if dil == 1 else o)
            la.append(lse[0] if dil == 1 else lse)
        cw = wts["cmp"]
        kc, vct = _compress(x16, C16_CMP // LANES, cw["w1t"][l], cw["w1b"][l], cw["ptop"][l], cw["pbot"][l], cw["w2d"][l],
                            cw["w2t"][l], nb, nchunk)
        ocmp, sel, cnt = _nsa_select(h1, kc[0], vct[1], cst["ovt"], cst["d0"], cst["cb"], nb, seq)
        flags = (cnt[:, :, :, 0, :].reshape(nb, N_KV, nq, nq, 2).sum(-1) > 0).astype(jnp.int32).reshape(-1)
        osel = _nsa_selected(flags, h1, v1t, sel, cst["qfeat"], cst["kfeat"], cst["dmask"], nb, seq).reshape(ntok, 4 * LANES)
        owin = _banded_attention(h3, C_BQ // 512, h3, C_BWK // (2 * LANES), 2 * LANES, v1t, R_BWV // LANES, LANES,
                                 cst["bw_bias"], cst["bw_ml"], nb, seq, False)[0]
        oc = _banded_attention(h3, C_CQ // 512, h3, C_CK // (2 * LANES), 2 * LANES, v1t, R_CV // LANES, LANES,
                               cst["c_bias"], cst["c_ml"][l], nb, seq, False)[0]
        xf, xb = _merge(oa, la, ocmp, osel, owin, oc, h1, xf, wts["wb"][l], wts["wo"][l],
                        ln_g[l][None, :], ln_b[l][None, :], cst["expand"], alpha)
    return xf.reshape(nb, seq, dm).astype(x.dtype)
```

```python
import functools
import math

import ml_dtypes
import numpy as np
import jax
import jax.numpy as jnp
from jax import lax
from jax.experimental import pallas as pl
from jax.experimental.pallas import tpu as pltpu

HD = 64
TILE = 128
LANES = 128
N_HEADS = 8
N_KV = 2
GQA = N_HEADS // N_KV
A_PATTERNS = ((128, 1), (512, 4), (2048, 16))
CMP_BLK, CMP_STRIDE, CMP_HIDDEN = 32, 16, 256
SEL_BLK, N_SEL, N_FORCED = 64, 16, 3
NSA_WINDOW, C_WINDOW = 512, 128
Q_TILES = 4
ONES_ROWS = 16
SEL_WIDE = 4
LN_EPS = 1e-5
NEG = -1e30
ATTN_SCALE = HD ** -0.5
LOG2E = math.log2(math.e)
LN2 = math.log(2.0)
Q_SCALE = ATTN_SCALE * LOG2E
VMEM_LIMIT = 56 * 1024 * 1024

C_MG, C_AQ0, C_AK0, C_BQ, C_CQ = 0, 3072, 3584, 4096, 4608
C_AG, C_BG, C_CG = 5120, 5632, 6144
C_BSK, C_BWK, C_CK, C_BGATE = 6656, 6912, 7168, 7424
H1_COLS = 8192
C16_CMP = 1024
R_AV0, R_BSV, R_BWV, R_CV, V1_ROWS = 0, 512, 640, 768, 896

_BF = jnp.bfloat16
_F32 = jnp.float32
_NT = (((1,), (1,)), ((), ()))


def _cparams(sem):
    return pltpu.CompilerParams(dimension_semantics=sem, vmem_limit_bytes=VMEM_LIMIT)


def _mm_kernel(x_ref, w_ref, b_ref, o_ref):
    acc = jnp.dot(x_ref[...], w_ref[...], preferred_element_type=_F32)
    o_ref[...] = (acc + b_ref[...]).astype(o_ref.dtype)


def _mm_t_kernel(x_ref, wt_ref, bt_ref, o_ref):
    acc = lax.dot_general(wt_ref[...], x_ref[...], _NT, preferred_element_type=_F32)
    o_ref[...] = (acc + bt_ref[...]).astype(o_ref.dtype)


def _fold_rows(x_refs, xs_ref, r):
    n = xs_ref.shape[0] // r
    for j, x_ref in enumerate(x_refs):
        for c in range(r):
            xs_ref[c * n:(c + 1) * n, j * LANES:(j + 1) * LANES] = x_ref[pl.ds(c, n, stride=r), :].astype(xs_ref.dtype)
    return n


def _mm_fold_kernel(*refs, r):
    (w_ref, b_ref, o_ref, xs_ref), x_refs = refs[-4:], refs[:-4]
    n = _fold_rows(x_refs, xs_ref, r)
    acc = jnp.dot(xs_ref[...], w_ref[...], preferred_element_type=_F32) + b_ref[...]
    for c in range(r):
        o_ref[c] = acc[c * n:(c + 1) * n, :].astype(o_ref.dtype)


def _mm_t_fold_kernel(*refs, r):
    (wt_ref, bt_ref, o_ref, xs_ref), x_refs = refs[-4:], refs[:-4]
    n = _fold_rows(x_refs, xs_ref, r)
    acc = lax.dot_general(wt_ref[...], xs_ref[...], _NT, preferred_element_type=_F32) + bt_ref[...]
    for c in range(r):
        o_ref[c] = acc[:, c * n:(c + 1) * n].astype(o_ref.dtype)


def _matmul(x, w, b, tm=1024, tn=1024):
    m, k = x.shape
    n = w.shape[1]
    tn = min(tn, n)
    return pl.pallas_call(
        _mm_kernel,
        grid=(m // tm, n // tn),
        in_specs=[pl.BlockSpec((tm, k), lambda i, j: (i, 0)),
                  pl.BlockSpec((k, tn), lambda i, j: (0, j)),
                  pl.BlockSpec((1, tn), lambda i, j: (0, j))],
        out_specs=pl.BlockSpec((tm, tn), lambda i, j: (i, j)),
        out_shape=jax.ShapeDtypeStruct((m, n), _BF),
        compiler_params=_cparams(("parallel", "arbitrary")),
        name="proj_tok",
    )(x, w, b)


def _matmul_t(x, wt, bt, tm=1024):
    m, k = x.shape
    n = wt.shape[0]
    return pl.pallas_call(
        _mm_t_kernel,
        grid=(m // tm,),
        in_specs=[pl.BlockSpec((tm, k), lambda i: (i, 0)),
                  pl.BlockSpec((n, k), lambda i: (0, 0)),
                  pl.BlockSpec((n, 1), lambda i: (0, 0))],
        out_specs=pl.BlockSpec((n, tm), lambda i: (0, i)),
        out_shape=jax.ShapeDtypeStruct((n, m), _BF),
        compiler_params=_cparams(("parallel",)),
        name="proj_feat",
    )(x, wt, bt)


def _matmul_fold(x, w, b, r, tm=1024):
    m, k = x.shape
    n = w.shape[1]
    return pl.pallas_call(
        functools.partial(_mm_fold_kernel, r=r),
        grid=(m // tm,),
        in_specs=[pl.BlockSpec((tm, LANES), functools.partial(lambda i, j: (i, j), j=j)) for j in range(k // LANES)]
                 + [pl.BlockSpec((k, n), lambda i: (0, 0)),
                  pl.BlockSpec((1, n), lambda i: (0, 0))],
        out_specs=pl.BlockSpec((r, tm // r, n), lambda i: (0, i, 0)),
        out_shape=jax.ShapeDtypeStruct((r, m // r, n), _BF),
        scratch_shapes=[pltpu.VMEM((tm, k), _BF)],
        compiler_params=_cparams(("parallel",)),
        name="proj_tok_fold",
    )(*([x] * (k // LANES)), w, b)


def _matmul_t_fold(x, wt, bt, r, tm=2048):
    m, k = x.shape
    n = wt.shape[0]
    return pl.pallas_call(
        functools.partial(_mm_t_fold_kernel, r=r),
        grid=(m // tm,),
        in_specs=[pl.BlockSpec((tm, LANES), functools.partial(lambda i, j: (i, j), j=j)) for j in range(k // LANES)]
                 + [pl.BlockSpec((n, k), lambda i: (0, 0)),
                  pl.BlockSpec((n, 1), lambda i: (0, 0))],
        out_specs=pl.BlockSpec((r, n, tm // r), lambda i: (0, 0, i)),
        out_shape=jax.ShapeDtypeStruct((r, n, m // r), _BF),
        scratch_shapes=[pltpu.VMEM((tm, k), _BF)],
        compiler_params=_cparams(("parallel",)),
        name="proj_feat_fold",
    )(*([x] * (k // LANES)), wt, bt)


def _stack_queries(q_ref, row0, heads):
    lane = lax.broadcasted_iota(jnp.int32, (TILE, LANES), 1)
    parts = []
    for h in heads:
        pair, half = divmod(h, 2)
        qp = q_ref[row0:row0 + TILE, pair * LANES:(pair + 1) * LANES]
        keep = (lane >= HD) if half else (lane < HD)
        parts.append(jnp.where(keep, qp, jnp.zeros_like(qp)))
    return parts[0] if len(parts) == 1 else jnp.concatenate(parts, axis=0)


def _probs(s, m):
    return jnp.exp2((s - m).astype(_BF))


def _with_ones(vt):
    return jnp.concatenate([vt, jnp.ones((ONES_ROWS, vt.shape[1]), vt.dtype)], axis=0)


def _run_skewed(unit_stages):
    nstage = max(len(s) for s in unit_stages)
    for t in range(len(unit_stages) + nstage - 1):
        for u in range(min(t, len(unit_stages) - 1), -1, -1):
            if t - u < len(unit_stages[u]):
                unit_stages[u][t - u]()


def _banded_kernel(*refs, n_delta, n_blk, units, heads_per_unit, shared_values, want_lse):
    q_ref = refs[0]
    k_refs = refs[1:1 + n_blk]
    v_refs = refs[1 + n_blk:1 + 2 * n_blk]
    bias_ref, ml_ref = refs[1 + 2 * n_blk:3 + 2 * n_blk]
    outs = refs[3 + 2 * n_blk:]
    o_ref = outs[0]
    lse_ref = outs[1] if want_lse else None
    ot_ref = outs[2] if want_lse else outs[1]
    lt_ref = outs[3] if want_lse else None
    step = pl.program_id(2)
    g = heads_per_unit
    vr = HD if shared_values else g * HD

    def unit(a, u):
        st = {}
        heads = [u * g + j for j in range(g)]

        def key_tile(d):
            blocks_back, half = divmod(d - a + Q_TILES - 1, Q_TILES)
            return blocks_back, (Q_TILES - 1 - half) * TILE

        def scores():
            qs = _stack_queries(q_ref, a * TILE, heads)
            kts = []
            for d in range(n_delta):
                blk, r0 = key_tile(d)
                kts.append(k_refs[blk][r0:r0 + TILE, u * LANES:(u + 1) * LANES])
            s_all = lax.dot_general(jnp.concatenate(kts, axis=0), qs, _NT, preferred_element_type=_F32)
            ss = []
            for d in range(n_delta):
                s = s_all[d * TILE:(d + 1) * TILE, :] + bias_ref[u, d]
                if d > a:
                    s = s + jnp.where(step * Q_TILES + a < d, NEG, 0.0)
                ss.append(s)
            st["s"] = ss

        def softmax():
            ss = st.pop("s")
            m0 = ml_ref[u, 0:1, :]
            m = m0
            for s in ss:
                m = jnp.maximum(m, jnp.max(s, axis=0, keepdims=True))
            st["p"], st["m"], st["l0"] = [_probs(s, m) for s in ss], m, ml_ref[u, 1:2, :] * jnp.exp2(m0 - m)

        def values():
            ps, m, l0 = st.pop("p"), st.pop("m"), st.pop("l0")
            acc = None
            for d in range(n_delta):
                blk, r0 = key_tile(d)
                vt = _with_ones(v_refs[blk][u * vr:(u + 1) * vr, r0:r0 + TILE])
                part = jnp.dot(vt, ps[d], preferred_element_type=_F32)
                acc = part if acc is None else acc + part
            l = l0 + acc[vr:vr + 1, :]
            o = acc[0:vr, :] * (1.0 / l)
            lse = (m + jnp.log2(l)) * LN2
            for j, h in enumerate(heads):
                r0 = 0 if shared_values else j * HD
                ot_ref[h * HD:(h + 1) * HD, a * TILE:(a + 1) * TILE] = o[r0:r0 + HD, j * TILE:(j + 1) * TILE]
                if want_lse:
                    lt_ref[h * HD:(h + 1) * HD, a * TILE:(a + 1) * TILE] = jnp.broadcast_to(
                        lse[:, j * TILE:(j + 1) * TILE], (HD, TILE))

        return [scores, softmax, values]

    _run_skewed([unit(a, u) for a in range(Q_TILES) for u in range(units)])
    o_ref[...] = ot_ref[...].T.astype(o_ref.dtype)
    if want_lse:
        lse_ref[...] = lt_ref[...].T


def _banded_bias(slopes, window, dist_scale, n_delta, heads_per_unit):
    kk = np.arange(TILE)[:, None]
    qq = np.arange(TILE)[None, :]
    out = []
    slopes = np.asarray(slopes, np.float64).reshape(-1, heads_per_unit)
    for unit_slopes in slopes:
        per_d = []
        for d in range(n_delta):
            dist = qq - kk + d * TILE
            ok = (dist >= 0) & (dist <= window)
            per_d.append(np.concatenate([np.where(ok, -s * LOG2E * dist_scale * dist, NEG) for s in unit_slopes], axis=1))
        out.append(np.stack(per_d))
    return jnp.asarray(np.stack(out), _F32)


def _banded_attention(q_arr, q_cb, k_arr, k_cb, k_lanes, vt_arr, v_rb, v_rows, bias, ml, nbatch, seq, want_lse,
                      nclass=1):
    nrow = nbatch * seq
    rows = Q_TILES * TILE
    nq = seq // rows
    units, n_delta = bias.shape[0], bias.shape[1]
    n_blk = 1 + -(-(n_delta - 1) // Q_TILES)
    g = N_HEADS // units
    sq = pl.Squeezed()

    def kmap(o):
        return lambda b, c, i: (c, b * nq + jnp.maximum(i - o, 0), k_cb)

    def vmap_(o):
        return lambda b, c, i: (c, v_rb, b * nq + jnp.maximum(i - o, 0))

    in_specs = [pl.BlockSpec((sq, rows, 4 * LANES), lambda b, c, i: (c, b * nq + i, q_cb))]
    in_specs += [pl.BlockSpec((sq, rows, k_lanes), kmap(o)) for o in range(n_blk)]
    in_specs += [pl.BlockSpec((sq, v_rows, rows), vmap_(o)) for o in range(n_blk)]
    in_specs += [pl.BlockSpec(bias.shape, lambda b, c, i: (0, 0, 0, 0)),
                 pl.BlockSpec(ml.shape, lambda b, c, i: (0, 0, 0))]
    out_spec = pl.BlockSpec((sq, rows, 4 * LANES), lambda b, c, i: (c, b * nq + i, 0))
    out_shape = [jax.ShapeDtypeStruct((nclass, nrow, 4 * LANES), _BF)]
    out_specs = [out_spec]
    scratch = [pltpu.VMEM((4 * LANES, rows), _F32)]
    if want_lse:
        out_shape.append(jax.ShapeDtypeStruct((nclass, nrow, 4 * LANES), _F32))
        out_specs.append(out_spec)
        scratch.append(pltpu.VMEM((4 * LANES, rows), _F32))
    kern = functools.partial(_banded_kernel, n_delta=n_delta, n_blk=n_blk, units=units, heads_per_unit=g,
                             shared_values=v_rows == units * HD, want_lse=want_lse)
    res = pl.pallas_call(
        kern, grid=(nbatch, nclass, nq), in_specs=in_specs, out_specs=out_specs, out_shape=out_shape,
        scratch_shapes=scratch, compiler_params=_cparams(("parallel", "parallel", "arbitrary")),
        name=f"banded_u{units}_d{n_delta}",
    )(q_arr, *([k_arr] * n_blk), *([vt_arr] * n_blk), bias, ml)
    return res if want_lse else res[0]


def _compress_kernel(x_ref, w1t_ref, w1b_ref, ptop_ref, pbot_ref, w2d_ref, w2t_ref, kc_ref, vct_ref):
    nchunk = x_ref.shape[1]
    c = jnp.concatenate([x_ref[p] for p in range(CMP_STRIDE)], axis=1).astype(_F32)
    top = jnp.dot((c + ptop_ref[...]).astype(_BF), w1t_ref[...], preferred_element_type=_F32)
    bot = jnp.dot((c + pbot_ref[...]).astype(_BF), w1b_ref[...], preferred_element_type=_F32)
    hid = top + pltpu.roll(bot, nchunk - 1, 0)
    act = jax.nn.gelu(hid, approximate=True).astype(_BF)
    kc_ref[...] = jnp.dot(act, w2d_ref[...], preferred_element_type=_F32).astype(kc_ref.dtype)
    vct_ref[...] = lax.dot_general(w2t_ref[...], act, _NT, preferred_element_type=_F32).astype(vct_ref.dtype)


def _compress(x16, col0, w1t, w1b, ptop, pbot, w2d, w2t, nb, nchunk):
    two, nkv = 2, N_KV
    width = CMP_STRIDE * LANES
    sq = pl.Squeezed()
    return pl.pallas_call(
        _compress_kernel,
        grid=(two, nb, nkv),
        in_specs=[pl.BlockSpec((CMP_STRIDE, nchunk, LANES), lambda t, b, h: (0, b, col0 + t * nkv + h)),
                  pl.BlockSpec((sq, width, CMP_HIDDEN), lambda t, b, h: (t, 0, 0)),
                  pl.BlockSpec((sq, width, CMP_HIDDEN), lambda t, b, h: (t, 0, 0)),
                  pl.BlockSpec((sq, 1, width), lambda t, b, h: (t, 0, 0)),
                  pl.BlockSpec((sq, 1, width), lambda t, b, h: (t, 0, 0)),
                  pl.BlockSpec((sq, CMP_HIDDEN, 2 * HD), lambda t, b, h: (t, 0, 0)),
                  pl.BlockSpec((sq, HD, CMP_HIDDEN), lambda t, b, h: (t, 0, 0))],
        out_specs=[pl.BlockSpec((sq, sq, sq, nchunk, 2 * HD), lambda t, b, h: (t, b, h, 0, 0)),
                   pl.BlockSpec((sq, sq, sq, HD, nchunk), lambda t, b, h: (t, b, h, 0, 0))],
        out_shape=[jax.ShapeDtypeStruct((two, nb, nkv, nchunk, 2 * HD), _BF),
                   jax.ShapeDtypeStruct((two, nb, nkv, HD, nchunk), _BF)],
        compiler_params=_cparams(("parallel", "parallel", "parallel")),
        name="nsa_compress",
    )(x16, w1t, w1b, ptop, pbot, w2d, w2t)


def _select_kernel(q_ref, kc_ref, vct_ref, ovt_ref, d0_ref, cb_ref, ocmp_ref, sel_ref, cnt_ref, ot_ref):
    i = pl.program_id(1)
    t0 = (i * TILE).astype(_F32)
    qs = [_stack_queries(q_ref, 0, [hkv * GQA + j for j in range(GQA)]) for hkv in range(N_KV)]
    nsel = ovt_ref.shape[0]
    nchunk = kc_ref.shape[1] // TILE

    def compressed(rows):
        def one(hkv):
            s = (lax.dot_general(kc_ref[hkv, 0:rows, :], qs[hkv], _NT, preferred_element_type=_F32)
                 + cb_ref[hkv, 0:rows, :])
            sb = jnp.where(d0_ref[0:rows, :] + t0 >= 0, s, NEG)
            m = jnp.max(sb, axis=0, keepdims=True)
            e = jnp.exp2(sb - m)
            den = jnp.sum(e, axis=0, keepdims=True)
            p = e * jnp.where(m > 0.5 * NEG, 1.0 / den, 0.0)
            o = jnp.dot(vct_ref[hkv, :, 0:rows], p.astype(_BF), preferred_element_type=_F32)
            psum = p[:, 0:TILE]
            for j in range(1, GQA):
                psum = psum + p[:, j * TILE:(j + 1) * TILE]
            hi = psum.astype(_BF)
            lo = (psum - hi.astype(_F32)).astype(_BF)
            imp = (jnp.dot(ovt_ref[:, 0:rows], hi, preferred_element_type=_F32)
                   + jnp.dot(ovt_ref[:, 0:rows], lo, preferred_element_type=_F32))
            return o, imp
        return lambda: tuple(one(hkv) for hkv in range(N_KV))

    last_block = (i * TILE + TILE - CMP_BLK) // CMP_STRIDE
    res = lax.switch(jnp.minimum(last_block // TILE, nchunk - 1),
                     [compressed((c + 1) * TILE) for c in range(nchunk)])
    for hkv, (o, _) in enumerate(res):
        for j in range(GQA):
            h = hkv * GQA + j
            ot_ref[h * HD:(h + 1) * HD, :] = o[:, j * TILE:(j + 1) * TILE]
    ocmp_ref[...] = ot_ref[...].T.astype(ocmp_ref.dtype)
    jf = lax.broadcasted_iota(jnp.int32, (nsel, TILE), 0).astype(_F32)
    qlane = lax.broadcasted_iota(jnp.int32, (nsel, TILE), 1)
    cur = (i * (TILE // SEL_BLK)).astype(_F32) + jnp.where(qlane >= SEL_BLK, 1.0, 0.0)
    forced = (jf == 0.0) | (jf == cur) | (jf == cur - 1.0)
    free = (jf <= cur) & jnp.logical_not(forced)
    sel0 = jnp.where(forced, 1.0, 0.0)
    init = tuple((jnp.where(free, imp, -2.0), sel0) for _, imp in res)

    def pick(carry):
        work, sel = carry
        mx = jnp.max(work, axis=0, keepdims=True)
        first = jnp.min(jnp.where(work == mx, jf, 1e9), axis=0, keepdims=True)
        hit = (jf == first) & (mx >= 0.0)
        return jnp.where(hit, -2.0, work), jnp.where(hit, 1.0, sel)

    done = lax.fori_loop(0, N_SEL - N_FORCED, lambda _, cs: tuple(pick(c) for c in cs), init)
    ones = jnp.ones((8, TILE), _BF)
    for hkv, (_, sel) in enumerate(done):
        sel_ref[hkv] = sel
        cnt_ref[hkv] = lax.dot_general(ones, sel.astype(_BF), _NT, preferred_element_type=_F32)


def _nsa_select(h1, kc, vct, ovt, d0, cb, nb, seq):
    nq = seq // TILE
    ncmp = kc.shape[2]
    nsel = seq // SEL_BLK
    sq = pl.Squeezed()
    qcb = C_BQ // (4 * LANES)
    return pl.pallas_call(
        _select_kernel,
        grid=(nb, nq),
        in_specs=[pl.BlockSpec((TILE, 4 * LANES), lambda b, i: (b * nq + i, qcb)),
                  pl.BlockSpec((sq, N_KV, ncmp, 2 * HD), lambda b, i: (b, 0, 0, 0)),
                  pl.BlockSpec((sq, N_KV, HD, ncmp), lambda b, i: (b, 0, 0, 0)),
                  pl.BlockSpec((nsel, ncmp), lambda b, i: (0, 0)),
                  pl.BlockSpec((ncmp, GQA * TILE), lambda b, i: (0, 0)),
                  pl.BlockSpec((N_KV, ncmp, GQA * TILE), lambda b, i: (0, 0, 0))],
        out_specs=[pl.BlockSpec((TILE, 4 * LANES), lambda b, i: (b * nq + i, 0)),
                   pl.BlockSpec((sq, N_KV, sq, nsel, TILE), lambda b, i: (b, 0, i, 0, 0)),
                   pl.BlockSpec((sq, N_KV, sq, 8, nsel), lambda b, i: (b, 0, i, 0, 0))],
        out_shape=[jax.ShapeDtypeStruct((nb * seq, 4 * LANES), _BF),
                   jax.ShapeDtypeStruct((nb, N_KV, nq, nsel, TILE), _F32),
                   jax.ShapeDtypeStruct((nb, N_KV, nq, 8, nsel), _F32)],
        scratch_shapes=[pltpu.VMEM((4 * LANES, TILE), _F32)],
        compiler_params=_cparams(("parallel", "arbitrary")),
        name="nsa_select",
    )(h1, kc, vct, ovt, d0, cb)


def _selected_kernel(*refs, nq, nprob):
    flag_ref, q_ref, k_ref = refs[0:3]
    vt_refs = refs[3:3 + nprob]
    sel_ref, qfeat_ref, kfeat_ref, dmask_ref, o_ref, ot_ref, list_ref = refs[3 + nprob:]
    hkv = pl.program_id(0)
    i = pl.program_id(1)
    n = GQA * TILE
    lstride = nq + 2 * SEL_WIDE
    sub = lax.broadcasted_iota(jnp.int32, (8, TILE), 0)

    def problem(pb):
        fbase = ((pb * N_KV + hkv) * nq + i) * nq
        lbase = pb * lstride
        qa = jnp.concatenate([_stack_queries(q_ref.at[pb], 0, list(range(GQA))), qfeat_ref[hkv]], axis=1)

        def scores(jjs):
            rows = []
            for jj in jjs:
                kt = k_ref[pb, pl.ds(pl.multiple_of(jj * TILE, TILE), TILE), :]
                tiles_back = ((jj - i) * TILE).astype(_F32)
                kf = (kfeat_ref[0] + tiles_back * kfeat_ref[1]).astype(_BF)
                rows.append(jnp.concatenate([kt, kf], axis=1))
            s = lax.dot_general(jnp.concatenate(rows, axis=0), qa, _NT, preferred_element_type=_F32)
            return [s[w * TILE:(w + 1) * TILE, :] for w in range(len(jjs))]

        def scan(jj, cnt):
            list_ref[lbase + cnt] = jj
            return cnt + (flag_ref[fbase + jj] > 0).astype(jnp.int32)

        n_act = lax.fori_loop(0, i, scan, jnp.int32(0))
        for w in range(SEL_WIDE):
            list_ref[lbase + n_act + w] = 0

        def sel_row(blk8, r):
            row = jnp.max(jnp.where(sub == r, blk8, 0.0), axis=0, keepdims=True)
            return jnp.concatenate([row] * GQA, axis=1)

        def list_mask(idx):
            jj = list_ref[lbase + idx]
            ok = idx < n_act
            blk8 = sel_ref[pb, pl.ds(pl.multiple_of((jj // 4) * 8, 8), 8), :]
            r = jnp.where(ok, (jj % 4) * 2, -2)
            r0 = jnp.where(sel_row(blk8, r) > 0, 0.0, NEG)
            r1 = jnp.where(sel_row(blk8, r + 1) > 0, 0.0, NEG)
            return jj, jnp.concatenate([jnp.broadcast_to(r0, (SEL_BLK, n)), jnp.broadcast_to(r1, (SEL_BLK, n))], axis=0)

        def masked_scores(idxs, diag=False):
            entries = [list_mask(idx) for idx in idxs]
            if diag:
                entries = [(i, dmask_ref[...])] + entries
            ss = scores([jj for jj, _ in entries])
            return [(s + bias, jj) for s, (jj, bias) in zip(ss, entries)]

        def update(state, tiles):
            m, acc = state
            mn = m
            for s, _ in tiles:
                mn = jnp.maximum(mn, jnp.max(s, axis=0, keepdims=True))
            alpha = jnp.exp2(m - mn)
            vt = jnp.concatenate([vt_refs[pb][:, pl.ds(pl.multiple_of(jj * TILE, TILE), TILE)] for _, jj in tiles], axis=1)
            p = jnp.concatenate([_probs(s, mn) for s, _ in tiles], axis=0)
            pv = jnp.dot(_with_ones(vt), p, preferred_element_type=_F32)
            return mn, alpha * acc + pv

        def first(state):
            return update(state, masked_scores(range(SEL_WIDE - 1), diag=True))

        def group(t, state):
            base = SEL_WIDE - 1 + t * SEL_WIDE
            return update(state, masked_scores([base + w for w in range(SEL_WIDE)]))

        ngroups = (jnp.maximum(n_act - (SEL_WIDE - 1), 0) + SEL_WIDE - 1) // SEL_WIDE
        return first, group, ngroups

    probs = [problem(pb) for pb in range(nprob)]
    init = (jnp.full((1, n), NEG, _F32), jnp.zeros((HD + ONES_ROWS, n), _F32))
    states = tuple(first(init) for first, _, _ in probs)
    ngroups = probs[0][2]
    for _, _, ng in probs[1:]:
        ngroups = jnp.maximum(ngroups, ng)
    states = lax.fori_loop(0, ngroups, lambda t, sts: tuple(group(t, st) for (_, group, _), st in zip(probs, sts)),
                           states)
    for pb, (m, acc) in enumerate(states):
        o = acc[0:HD, :] * (1.0 / acc[HD:HD + 1, :])
        for j in range(GQA):
            ot_ref[j * HD:(j + 1) * HD, :] = o[:, j * TILE:(j + 1) * TILE]
        o_ref[pb] = ot_ref[...].T.astype(o_ref.dtype)


def _nsa_selected(flags, h1, v1t, sel, qfeat, kfeat, dmask, nb, seq):
    nq = seq // TILE
    nsel = seq // SEL_BLK
    sq = pl.Squeezed()
    qcb = C_BQ // (2 * LANES)
    kcb = C_BSK // LANES
    vrb = R_BSV // HD
    h3 = h1.reshape(nb, seq, h1.shape[-1])
    grid_spec = pltpu.PrefetchScalarGridSpec(
        num_scalar_prefetch=1,
        grid=(N_KV, nq),
        in_specs=[pl.BlockSpec((nb, TILE, 2 * LANES), lambda h, i, f: (0, i, qcb + h)),
                  pl.BlockSpec((nb, seq, LANES), lambda h, i, f: (0, 0, kcb + h))]
                 + [pl.BlockSpec((sq, HD, seq), functools.partial(lambda h, i, f, b: (0, vrb + h, b), b=b))
                    for b in range(nb)]
                 + [pl.BlockSpec((nb, sq, sq, nsel, TILE), lambda h, i, f: (0, h, i, 0, 0)),
                    pl.BlockSpec((N_KV, GQA * TILE, LANES), lambda h, i, f: (0, 0, 0)),
                    pl.BlockSpec((2, TILE, LANES), lambda h, i, f: (0, 0, 0)),
                    pl.BlockSpec((TILE, GQA * TILE), lambda h, i, f: (0, 0))],
        out_specs=pl.BlockSpec((nb, TILE, 2 * LANES), lambda h, i, f: (0, i, h)),
        scratch_shapes=[pltpu.VMEM((2 * LANES, TILE), _F32), pltpu.SMEM((nb * (nq + 2 * SEL_WIDE),), jnp.int32)],
    )
    return pl.pallas_call(
        functools.partial(_selected_kernel, nq=nq, nprob=nb),
        grid_spec=grid_spec,
        out_shape=jax.ShapeDtypeStruct((nb, seq, 4 * LANES), _BF),
        compiler_params=_cparams(("parallel", "arbitrary")),
        name="nsa_selected",
    )(flags, h3, h3, *([v1t] * nb), sel, qfeat, kfeat, dmask)


def _merge_kernel(oa0, oa1, oa2, la0, la1, la2, ocmp, osel, owin, oc, ag, bg, cg, gate, mg0, mg1, mg2,
                  x_ref, wb_ref, wo_ref, lng_ref, lnb_ref, ex_ref, xo_ref, xb_ref, *u_refs, alpha):
    f = lambda r: r[...].astype(_F32)
    sig = lambda v: 0.5 * jnp.tanh(0.5 * v) + 0.5
    silu = lambda v: v * sig(v)
    w = N_HEADS * HD
    sg = sig(f(gate))
    hi = sg.astype(_BF)
    lo = (sg - hi.astype(_F32)).astype(_BF)
    gx = (jnp.dot(hi, ex_ref[...], preferred_element_type=_F32)
          + jnp.dot(lo, ex_ref[...], preferred_element_type=_F32))
    def unfold(src, dst):
        r, n = src.shape[0], src.shape[1]
        for c in range(r):
            blk = src[c].astype(_F32)
            for j in range(dst.shape[0]):
                dst[j, pl.ds(c, n, stride=r), :] = blk[:, j * LANES:(j + 1) * LANES]
        return jnp.concatenate([dst[j] for j in range(dst.shape[0])], axis=1)

    o1, l1, o2, l2 = (unfold(src, dst) for src, dst in zip((oa1, la1, oa2, la2), u_refs))
    l0 = la0[...]
    mx = jnp.maximum(jnp.maximum(l0, l1), l2)
    e0, e1, e2 = jnp.exp(l0 - mx), jnp.exp(l1 - mx), jnp.exp(l2 - mx)
    ya = (e0 * f(oa0) + e1 * o1 + e2 * o2) * (1.0 / (e0 + e1 + e2))
    ya = ya * silu(ag[...])
    yb = gx[:, 0:w] * f(ocmp) + gx[:, w:2 * w] * f(osel) + gx[:, 2 * w:3 * w] * f(owin)
    yb = yb * silu(bg[...])
    yc = f(oc) * silu(cg[...])
    merged = (sig(mg0[...]) * jnp.dot(ya.astype(_BF), wb_ref[0], preferred_element_type=_F32)
              + sig(mg1[...]) * jnp.dot(yb.astype(_BF), wb_ref[1], preferred_element_type=_F32)
              + sig(mg2[...]) * jnp.dot(yc.astype(_BF), wb_ref[2], preferred_element_type=_F32))
    y = jnp.dot(merged.astype(_BF), wo_ref[...], preferred_element_type=_F32)
    z = alpha * x_ref[...] + y
    mu = jnp.mean(z, axis=-1, keepdims=True)
    zc = z - mu
    var = jnp.mean(zc * zc, axis=-1, keepdims=True)
    out = zc * lax.rsqrt(var + LN_EPS) * lng_ref[...] + lnb_ref[...]
    xo_ref[...] = out
    xb_ref[...] = out.astype(_BF)


def _merge(oa, la, ocmp, osel, owin, oc, h1, x, wb, wo, lng, lnb, expand, alpha, tm=256):
    ntok, dm = x.shape
    w = 4 * LANES
    tok = lambda c: pl.BlockSpec((tm, w), lambda i: (i, c))
    h1s = lambda col, width: pl.BlockSpec((tm, width), lambda i: (i, col // width))
    cls = lambda a: pl.BlockSpec((a.shape[0], tm // a.shape[0], w), lambda i: (0, i, 0))
    in_specs = ([tok(0), cls(oa[1]), cls(oa[2]), tok(0), cls(la[1]), cls(la[2])] + [tok(0)] * 4
                + [h1s(C_AG, w), h1s(C_BG, w), h1s(C_CG, w)]
                + [h1s(C_BGATE, LANES)]
                + [h1s(C_MG + j * dm, dm) for j in range(3)]
                + [pl.BlockSpec((tm, dm), lambda i: (i, 0)),
                   pl.BlockSpec(wb.shape, lambda i: (0, 0, 0)),
                   pl.BlockSpec(wo.shape, lambda i: (0, 0)),
                   pl.BlockSpec((1, dm), lambda i: (0, 0)),
                   pl.BlockSpec((1, dm), lambda i: (0, 0)),
                   pl.BlockSpec(expand.shape, lambda i: (0, 0))])
    return pl.pallas_call(
        functools.partial(_merge_kernel, alpha=alpha),
        grid=(ntok // tm,),
        in_specs=in_specs,
        out_specs=[pl.BlockSpec((tm, dm), lambda i: (i, 0))] * 2,
        out_shape=[jax.ShapeDtypeStruct((ntok, dm), _F32), jax.ShapeDtypeStruct((ntok, dm), _BF)],
        scratch_shapes=[pltpu.VMEM((w // LANES, tm, LANES), _F32)] * 4,
        compiler_params=_cparams(("parallel",)),
        name="merge_norm",
    )(*oa, *la, ocmp, osel, owin, oc, *([h1] * 7), x, wb, wo, lng, lnb, expand)


def _alibi(n):
    return 2.0 ** (-8.0 * np.arange(1, n + 1, dtype=np.float64) / n)


def _prep_weights(w_in, b_in, w_cmp1, w_cmp2, cmp_pos, w_branch, w_out):
    widths = (1536, 1536, 1536, 512, 512, 128, 128, 128, 128, 128, 128, 512, 24, 512, 128, 128, 512, 3072)
    names = ("aq", "ak", "av", "ag", "bq", "bck", "bcv", "bsk", "bsv", "bwk", "bwv", "bg", "bgate", "cq", "ck", "cv", "cg", "mg")
    off = dict(zip(names, np.cumsum((0,) + widths[:-1]).tolist()))

    def cols(sl, scale=1.0):
        w, b = w_in[:, :, sl], b_in[:, sl]
        return (w, b) if scale == 1.0 else (w * scale, b * scale)

    def rng(name, start, n):
        return slice(off[name] + start, off[name] + start + n)

    def cat(parts):
        return jnp.concatenate([p[0] for p in parts], axis=2), jnp.concatenate([p[1] for p in parts], axis=1)

    def dup(name):
        return cat([cols(rng(name, kv * HD, HD)) for kv in range(N_KV) for _ in range(2)])

    ngate = 3 * N_HEADS
    pad = H1_COLS - C_BGATE - ngate
    zeros = (jnp.zeros(w_in.shape[:2] + (pad,), w_in.dtype), jnp.zeros(b_in.shape[:1] + (pad,), b_in.dtype))
    tok = [cols(rng("mg", 0, 3072)), cols(rng("aq", 0, 512), Q_SCALE), cols(rng("ak", 0, 512)),
           cols(rng("bq", 0, 512), Q_SCALE), cols(rng("cq", 0, 512), Q_SCALE),
           cols(rng("ag", 0, 512)), cols(rng("bg", 0, 512)), cols(rng("cg", 0, 512)),
           dup("bsk"), dup("bwk"), dup("ck"),
           cols(rng("bgate", 0, ngate)), zeros]
    w1 = jnp.concatenate([t[0].astype(_BF) for t in tok], axis=2)
    b1 = jnp.concatenate([t[1] for t in tok], axis=1)[:, None, :]
    wqk, bqk, wvt, bvt = [], [], [], []
    for gi in (1, 2):
        wq, bq = cols(rng("aq", gi * 512, 512), Q_SCALE)
        wk, bk = cols(rng("ak", gi * 512, 512))
        extra = []
        if A_PATTERNS[gi][1] == CMP_STRIDE:
            zw, zb = jnp.zeros(w_in.shape[:2] + (HD,), w_in.dtype), jnp.zeros(b_in.shape[:1] + (HD,), b_in.dtype)
            for name in ("bck", "bcv"):
                for kv in range(N_KV):
                    extra += [cols(rng(name, kv * HD, HD)), (zw, zb)]
        wqk.append(jnp.concatenate([wq, wk] + [e[0] for e in extra], axis=2).astype(_BF))
        bqk.append(jnp.concatenate([bq, bk] + [e[1] for e in extra], axis=1)[:, None, :])
        wv, bv = cols(rng("av", gi * 512, 512))
        wvt.append(wv.transpose(0, 2, 1).astype(_BF))
        bvt.append(bv[:, :, None])
    feat = [cols(rng("av", 0, 512)), cols(rng("bsv", 0, 128)), cols(rng("bwv", 0, 128)), cols(rng("cv", 0, 128))]
    wv1t = jnp.concatenate([t[0] for t in feat], axis=2).transpose(0, 2, 1).astype(_BF)
    bv1t = jnp.concatenate([t[1] for t in feat], axis=1)[:, :, None]
    depth = w_cmp1.shape[0]
    w1p = jnp.pad(w_cmp1.reshape(depth, 2, CMP_BLK, HD, CMP_HIDDEN), ((0, 0),) * 3 + ((0, LANES - HD), (0, 0)))
    w1p = w1p.reshape(depth, 2, 2, CMP_STRIDE * LANES, CMP_HIDDEN).astype(_BF)
    posp = jnp.pad(cmp_pos, ((0, 0),) * 3 + ((0, LANES - HD),)).reshape(depth, 2, 2, 1, CMP_STRIDE * LANES)
    cmpw = dict(w1t=w1p[:, :, 0], w1b=w1p[:, :, 1], ptop=posp[:, :, 0], pbot=posp[:, :, 1],
                w2d=jnp.concatenate([w_cmp2, w_cmp2], axis=-1).astype(_BF),
                w2t=w_cmp2.transpose(0, 1, 3, 2).astype(_BF))
    return dict(w1=w1, b1=b1, wqk=wqk, bqk=bqk, wvt=wvt, bvt=bvt, wv1t=wv1t, bv1t=bv1t, cmp=cmpw,
                wb=w_branch.astype(_BF), wo=w_out.astype(_BF))


def _constants(seq, sinks):
    a_slopes = _alibi(3 * N_HEADS).reshape(3, N_HEADS)
    bc_slopes = _alibi(N_HEADS)
    no_sink = lambda units, n: jnp.stack([jnp.full((units, n), NEG, _F32), jnp.zeros((units, n), _F32)], axis=1)
    c = {}
    c["a_bias"] = [_banded_bias(a_slopes[gi], win // dil, dil, 2, 2) for gi, (win, dil) in enumerate(A_PATTERNS)]
    c["a_ml"] = no_sink(N_HEADS // 2, 2 * TILE)
    c["bw_bias"] = _banded_bias(bc_slopes, NSA_WINDOW - 1, 1, 1 + NSA_WINDOW // TILE, GQA)
    c["bw_ml"] = no_sink(N_KV, GQA * TILE)
    c["c_bias"] = _banded_bias(bc_slopes, C_WINDOW - 1, 1, 2, GQA)
    sink_rows = jnp.repeat(sinks.astype(_F32).reshape(-1, N_KV, GQA) * LOG2E, TILE, axis=-1)
    c["c_ml"] = jnp.stack([sink_rows, jnp.ones_like(sink_rows)], axis=2)
    lane_slopes = np.repeat(bc_slopes.reshape(N_KV, GQA), TILE, axis=-1)
    ncmp = seq // CMP_STRIDE
    nsel = seq // SEL_BLK
    cs = np.arange(ncmp) * CMP_STRIDE
    ss = np.arange(nsel) * SEL_BLK
    overlap = (cs[None, :] < ss[:, None] + SEL_BLK) & (cs[None, :] + CMP_BLK > ss[:, None]) & (cs[None, :] + CMP_BLK <= seq)
    c["ovt"] = jnp.asarray(overlap, _BF)
    qq = np.tile(np.arange(TILE), GQA)[None, :]
    d0 = qq - (cs[:, None] + CMP_BLK - 1)
    c["d0"] = jnp.asarray(d0, _F32)
    c["cb"] = jnp.asarray(-(lane_slopes * LOG2E)[:, None, :] * d0[None], _F32)
    kk = np.arange(TILE)[:, None]
    c["dmask"] = jnp.asarray(np.where(kk <= qq, 0.0, NEG), _F32)
    rest = lane_slopes * LOG2E
    pieces = []
    for _ in range(3):
        piece = rest.astype(ml_dtypes.bfloat16).astype(np.float64)
        pieces.append(piece)
        rest = rest - piece
    qfeat = np.zeros((N_KV, GQA * TILE, LANES))
    kfeat = np.zeros((2, TILE, LANES))
    for a, piece in enumerate(pieces):
        qfeat[:, :, a] = piece
        qfeat[:, :, 3 + a] = piece
        kfeat[0, :, a] = np.arange(TILE)
        kfeat[1, :, 3 + a] = 1.0
    c["qfeat"] = jnp.asarray(qfeat, _BF)
    c["kfeat"] = jnp.asarray(kfeat, _F32)
    expand = np.zeros((LANES, 3 * N_HEADS * HD))
    for h in range(N_HEADS):
        for j in range(3):
            expand[h * 3 + j, j * N_HEADS * HD + h * HD:j * N_HEADS * HD + (h + 1) * HD] = 1.0
    c["expand"] = jnp.asarray(expand, _BF)
    return c


def kernel(x, w_in, b_in, w_cmp1, w_cmp2, cmp_pos, sinks, w_branch, w_out, ln_g, ln_b):
    nb, seq, dm = x.shape
    depth = w_in.shape[0]
    ntok = nb * seq
    alpha = (2 * depth) ** 0.25
    wts = _prep_weights(w_in, b_in, w_cmp1, w_cmp2, cmp_pos, w_branch, w_out)
    cst = _constants(seq, sinks)
    nchunk = seq // CMP_STRIDE
    nq = seq // TILE
    xf = x.reshape(ntok, dm)
    xb = xf.astype(_BF)
    for l in range(depth):
        h1 = _matmul(xb, wts["w1"][l], wts["b1"][l])
        v1t = _matmul_t(xb, wts["wv1t"][l], wts["bv1t"][l])[None]
        h3 = h1[None]
        oa, la = [], []
        for gi, (win, dil) in enumerate(A_PATTERNS):
            if dil == 1:
                qk, qcb, kcb, vt = h3, C_AQ0 // 512, C_AK0 // 512, v1t
            else:
                qk = _matmul_fold(xf, wts["wqk"][gi - 1][l], wts["bqk"][gi - 1][l], dil)
                vt = _matmul_t_fold(xf, wts["wvt"][gi - 1][l], wts["bvt"][gi - 1][l], dil)
                qcb, kcb = 0, 1
                if dil == CMP_STRIDE:
                    x16 = qk
            o, lse = _banded_attention(qk, qcb, qk, kcb, 4 * LANES, vt, R_AV0 // 512, 4 * LANES, cst["a_bias"][gi],
                                       cst["a_ml"], nb, seq // dil, True, nclass=dil)
            oa.append(o[0] if dil == 1 else o)
            la.append(lse[0] if dil == 1 else lse)
        cw = wts["cmp"]
        kc, vct = _compress(x16, C16_CMP // LANES, cw["w1t"][l], cw["w1b"][l], cw["ptop"][l], cw["pbot"][l], cw["w2d"][l],
                            cw["w2t"][l], nb, nchunk)
        ocmp, sel, cnt = _nsa_select(h1, kc[0], vct[1], cst["ovt"], cst["d0"], cst["cb"], nb, seq)
        flags = (cnt[:, :, :, 0, :].reshape(nb, N_KV, nq, nq, 2).sum(-1) > 0).astype(jnp.int32).reshape(-1)
        osel = _nsa_selected(flags, h1, v1t, sel, cst["qfeat"], cst["kfeat"], cst["dmask"], nb, seq).reshape(ntok, 4 * LANES)
        owin = _banded_attention(h3, C_BQ // 512, h3, C_BWK // (2 * LANES), 2 * LANES, v1t, R_BWV // LANES, LANES,
                                 cst["bw_bias"], cst["bw_ml"], nb, seq, False)[0]
        oc = _banded_attention(h3, C_CQ // 512, h3, C_CK // (2 * LANES), 2 * LANES, v1t, R_CV // LANES, LANES,
                               cst["c_bias"], cst["c_ml"][l], nb, seq, False)[0]
        xf, xb = _merge(oa, la, ocmp, osel, owin, oc, h1, xf, wts["wb"][l], wts["wo"][l],
                        ln_g[l][None, :], ln_b[l][None, :], cst["expand"], alpha)
    return xf.reshape(nb, seq, dm).astype(x.dtype)
```

```python
import functools
import math

import ml_dtypes
import numpy as np
import jax
import jax.numpy as jnp
from jax import lax
from jax.experimental import pallas as pl
from jax.experimental.pallas import tpu as pltpu

HD = 64
TILE = 128
LANES = 128
N_HEADS = 8
N_KV = 2
GQA = N_HEADS // N_KV
A_PATTERNS = ((128, 1), (512, 4), (2048, 16))
CMP_BLK, CMP_STRIDE, CMP_HIDDEN = 32, 16, 256
SEL_BLK, N_SEL, N_FORCED = 64, 16, 3
NSA_WINDOW, C_WINDOW = 512, 128
Q_TILES = 4
ONES_ROWS = 16
SEL_WIDE = 6
LN_EPS = 1e-5
NEG = -1e30
ATTN_SCALE = HD ** -0.5
LOG2E = math.log2(math.e)
LN2 = math.log(2.0)
Q_SCALE = ATTN_SCALE * LOG2E
VMEM_LIMIT = 56 * 1024 * 1024

C_MG, C_AQ0, C_AK0, C_BQ, C_CQ = 0, 3072, 3584, 4096, 4608
C_AG, C_BG, C_CG = 5120, 5632, 6144
C_BSK, C_BWK, C_CK, C_BGATE = 6656, 6912, 7168, 7424
H1_COLS = 7680
H1_TM, H1_TN = 2048, 1280
C16_CMP = 1024
R_AV0, R_BSV, R_BWV, R_CV, V1_ROWS = 0, 512, 640, 768, 896

_BF = jnp.bfloat16
_F32 = jnp.float32
_NT = (((1,), (1,)), ((), ()))


def _cparams(sem):
    return pltpu.CompilerParams(dimension_semantics=sem, vmem_limit_bytes=VMEM_LIMIT)


def _mm_kernel(x_ref, w_ref, b_ref, o_ref):
    acc = jnp.dot(x_ref[...], w_ref[...], preferred_element_type=_F32)
    o_ref[...] = (acc + b_ref[...]).astype(o_ref.dtype)


def _mm_t_kernel(x_ref, wt_ref, bt_ref, o_ref):
    acc = lax.dot_general(wt_ref[...], x_ref[...], _NT, preferred_element_type=_F32)
    o_ref[...] = (acc + bt_ref[...]).astype(o_ref.dtype)


def _fold_rows(x_refs, xs_ref, r):
    n = xs_ref.shape[0] // r
    for j, x_ref in enumerate(x_refs):
        for c in range(r):
            xs_ref[c * n:(c + 1) * n, j * LANES:(j + 1) * LANES] = x_ref[pl.ds(c, n, stride=r), :].astype(xs_ref.dtype)
    return n


def _mm_fold_kernel(*refs, r):
    (w_ref, b_ref, o_ref, xs_ref), x_refs = refs[-4:], refs[:-4]
    n = _fold_rows(x_refs, xs_ref, r)
    acc = jnp.dot(xs_ref[...], w_ref[...], preferred_element_type=_F32) + b_ref[...]
    for c in range(r):
        o_ref[c] = acc[c * n:(c + 1) * n, :].astype(o_ref.dtype)


def _mm_t_fold_kernel(*refs, r):
    (wt_ref, bt_ref, o_ref, xs_ref), x_refs = refs[-4:], refs[:-4]
    n = _fold_rows(x_refs, xs_ref, r)
    acc = lax.dot_general(wt_ref[...], xs_ref[...], _NT, preferred_element_type=_F32) + bt_ref[...]
    for c in range(r):
        o_ref[c] = acc[:, c * n:(c + 1) * n].astype(o_ref.dtype)


def _matmul(x, w, b, tm=1024, tn=1024):
    m, k = x.shape
    n = w.shape[1]
    tn = min(tn, n)
    return pl.pallas_call(
        _mm_kernel,
        grid=(m // tm, n // tn),
        in_specs=[pl.BlockSpec((tm, k), lambda i, j: (i, 0)),
                  pl.BlockSpec((k, tn), lambda i, j: (0, j)),
                  pl.BlockSpec((1, tn), lambda i, j: (0, j))],
        out_specs=pl.BlockSpec((tm, tn), lambda i, j: (i, j)),
        out_shape=jax.ShapeDtypeStruct((m, n), _BF),
        compiler_params=_cparams(("parallel", "arbitrary")),
        name="proj_tok",
    )(x, w, b)


def _matmul_t(x, wt, bt, tm=1024):
    m, k = x.shape
    n = wt.shape[0]
    return pl.pallas_call(
        _mm_t_kernel,
        grid=(m // tm,),
        in_specs=[pl.BlockSpec((tm, k), lambda i: (i, 0)),
                  pl.BlockSpec((n, k), lambda i: (0, 0)),
                  pl.BlockSpec((n, 1), lambda i: (0, 0))],
        out_specs=pl.BlockSpec((n, tm), lambda i: (0, i)),
        out_shape=jax.ShapeDtypeStruct((n, m), _BF),
        compiler_params=_cparams(("parallel",)),
        name="proj_feat",
    )(x, wt, bt)


def _matmul_fold(x, w, b, r, tm=1024):
    m, k = x.shape
    n = w.shape[1]
    return pl.pallas_call(
        functools.partial(_mm_fold_kernel, r=r),
        grid=(m // tm,),
        in_specs=[pl.BlockSpec((tm, LANES), functools.partial(lambda i, j: (i, j), j=j)) for j in range(k // LANES)]
                 + [pl.BlockSpec((k, n), lambda i: (0, 0)),
                  pl.BlockSpec((1, n), lambda i: (0, 0))],
        out_specs=pl.BlockSpec((r, tm // r, n), lambda i: (0, i, 0)),
        out_shape=jax.ShapeDtypeStruct((r, m // r, n), _BF),
        scratch_shapes=[pltpu.VMEM((tm, k), _BF)],
        compiler_params=_cparams(("parallel",)),
        name="proj_tok_fold",
    )(*([x] * (k // LANES)), w, b)


def _matmul_t_fold(x, wt, bt, r, tm=2048):
    m, k = x.shape
    n = wt.shape[0]
    return pl.pallas_call(
        functools.partial(_mm_t_fold_kernel, r=r),
        grid=(m // tm,),
        in_specs=[pl.BlockSpec((tm, LANES), functools.partial(lambda i, j: (i, j), j=j)) for j in range(k // LANES)]
                 + [pl.BlockSpec((n, k), lambda i: (0, 0)),
                  pl.BlockSpec((n, 1), lambda i: (0, 0))],
        out_specs=pl.BlockSpec((r, n, tm // r), lambda i: (0, 0, i)),
        out_shape=jax.ShapeDtypeStruct((r, n, m // r), _BF),
        scratch_shapes=[pltpu.VMEM((tm, k), _BF)],
        compiler_params=_cparams(("parallel",)),
        name="proj_feat_fold",
    )(*([x] * (k // LANES)), wt, bt)


def _stack_queries(q_ref, row0, heads):
    lane = lax.broadcasted_iota(jnp.int32, (TILE, LANES), 1)
    parts = []
    for h in heads:
        pair, half = divmod(h, 2)
        qp = q_ref[row0:row0 + TILE, pair * LANES:(pair + 1) * LANES]
        keep = (lane >= HD) if half else (lane < HD)
        parts.append(jnp.where(keep, qp, jnp.zeros_like(qp)))
    return parts[0] if len(parts) == 1 else jnp.concatenate(parts, axis=0)


def _probs(s, m):
    return jnp.exp2((s - m).astype(_BF))


def _with_ones(vt):
    return jnp.concatenate([vt, jnp.ones((ONES_ROWS, vt.shape[1]), vt.dtype)], axis=0)


def _run_skewed(unit_stages):
    nstage = max(len(s) for s in unit_stages)
    for t in range(len(unit_stages) + nstage - 1):
        for u in range(min(t, len(unit_stages) - 1), -1, -1):
            if t - u < len(unit_stages[u]):
                unit_stages[u][t - u]()


def _banded_kernel(*refs, n_delta, n_blk, units, heads_per_unit, shared_values, want_lse):
    q_ref = refs[0]
    k_refs = refs[1:1 + n_blk]
    v_refs = refs[1 + n_blk:1 + 2 * n_blk]
    bias_ref, ml_ref = refs[1 + 2 * n_blk:3 + 2 * n_blk]
    outs = refs[3 + 2 * n_blk:]
    o_ref = outs[0]
    lse_ref = outs[1] if want_lse else None
    ot_ref = outs[2] if want_lse else outs[1]
    lt_ref = outs[3] if want_lse else None
    step = pl.program_id(2)
    g = heads_per_unit
    vr = HD if shared_values else g * HD

    def unit(a, u):
        st = {}
        heads = [u * g + j for j in range(g)]

        def key_tile(d):
            blocks_back, half = divmod(d - a + Q_TILES - 1, Q_TILES)
            return blocks_back, (Q_TILES - 1 - half) * TILE

        def scores():
            qs = _stack_queries(q_ref, a * TILE, heads)
            kts = []
            for d in range(n_delta):
                blk, r0 = key_tile(d)
                kts.append(k_refs[blk][r0:r0 + TILE, u * LANES:(u + 1) * LANES])
            s_all = lax.dot_general(jnp.concatenate(kts, axis=0), qs, _NT, preferred_element_type=_F32)
            ss = []
            for d in range(n_delta):
                s = s_all[d * TILE:(d + 1) * TILE, :] + bias_ref[u, d]
                if d > a:
                    s = s + jnp.where(step * Q_TILES + a < d, NEG, 0.0)
                ss.append(s)
            st["s"] = ss

        def softmax():
            ss = st.pop("s")
            m0 = ml_ref[u, 0:1, :]
            m = m0
            for s in ss:
                m = jnp.maximum(m, jnp.max(s, axis=0, keepdims=True))
            st["p"], st["m"], st["l0"] = [_probs(s, m) for s in ss], m, ml_ref[u, 1:2, :] * jnp.exp2(m0 - m)

        def values():
            ps, m, l0 = st.pop("p"), st.pop("m"), st.pop("l0")
            acc = None
            for d in range(n_delta):
                blk, r0 = key_tile(d)
                vt = _with_ones(v_refs[blk][u * vr:(u + 1) * vr, r0:r0 + TILE])
                part = jnp.dot(vt, ps[d], preferred_element_type=_F32)
                acc = part if acc is None else acc + part
            l = l0 + acc[vr:vr + 1, :]
            o = acc[0:vr, :] * (1.0 / l)
            lse = (m + jnp.log2(l)) * LN2
            for j, h in enumerate(heads):
                r0 = 0 if shared_values else j * HD
                ot_ref[h * HD:(h + 1) * HD, a * TILE:(a + 1) * TILE] = o[r0:r0 + HD, j * TILE:(j + 1) * TILE]
                if want_lse:
                    lt_ref[h * HD:(h + 1) * HD, a * TILE:(a + 1) * TILE] = jnp.broadcast_to(
                        lse[:, j * TILE:(j + 1) * TILE], (HD, TILE))

        return [scores, softmax, values]

    _run_skewed([unit(a, u) for a in range(Q_TILES) for u in range(units)])
    o_ref[...] = ot_ref[...].T.astype(o_ref.dtype)
    if want_lse:
        lse_ref[...] = lt_ref[...].T


def _banded_bias(slopes, window, dist_scale, n_delta, heads_per_unit):
    kk = np.arange(TILE)[:, None]
    qq = np.arange(TILE)[None, :]
    out = []
    slopes = np.asarray(slopes, np.float64).reshape(-1, heads_per_unit)
    for unit_slopes in slopes:
        per_d = []
        for d in range(n_delta):
            dist = qq - kk + d * TILE
            ok = (dist >= 0) & (dist <= window)
            per_d.append(np.concatenate([np.where(ok, -s * LOG2E * dist_scale * dist, NEG) for s in unit_slopes], axis=1))
        out.append(np.stack(per_d))
    return jnp.asarray(np.stack(out), _F32)


def _banded_attention(q_arr, q_cb, k_arr, k_cb, k_lanes, vt_arr, v_rb, v_rows, bias, ml, nbatch, seq, want_lse,
                      nclass=1):
    nrow = nbatch * seq
    rows = Q_TILES * TILE
    nq = seq // rows
    units, n_delta = bias.shape[0], bias.shape[1]
    n_blk = 1 + -(-(n_delta - 1) // Q_TILES)
    g = N_HEADS // units
    sq = pl.Squeezed()

    def kmap(o):
        return lambda b, c, i: (c, b * nq + jnp.maximum(i - o, 0), k_cb)

    def vmap_(o):
        return lambda b, c, i: (c, v_rb, b * nq + jnp.maximum(i - o, 0))

    in_specs = [pl.BlockSpec((sq, rows, 4 * LANES), lambda b, c, i: (c, b * nq + i, q_cb))]
    in_specs += [pl.BlockSpec((sq, rows, k_lanes), kmap(o)) for o in range(n_blk)]
    in_specs += [pl.BlockSpec((sq, v_rows, rows), vmap_(o)) for o in range(n_blk)]
    in_specs += [pl.BlockSpec(bias.shape, lambda b, c, i: (0, 0, 0, 0)),
                 pl.BlockSpec(ml.shape, lambda b, c, i: (0, 0, 0))]
    out_spec = pl.BlockSpec((sq, rows, 4 * LANES), lambda b, c, i: (c, b * nq + i, 0))
    out_shape = [jax.ShapeDtypeStruct((nclass, nrow, 4 * LANES), _BF)]
    out_specs = [out_spec]
    scratch = [pltpu.VMEM((4 * LANES, rows), _F32)]
    if want_lse:
        out_shape.append(jax.ShapeDtypeStruct((nclass, nrow, 4 * LANES), _F32))
        out_specs.append(out_spec)
        scratch.append(pltpu.VMEM((4 * LANES, rows), _F32))
    kern = functools.partial(_banded_kernel, n_delta=n_delta, n_blk=n_blk, units=units, heads_per_unit=g,
                             shared_values=v_rows == units * HD, want_lse=want_lse)
    res = pl.pallas_call(
        kern, grid=(nbatch, nclass, nq), in_specs=in_specs, out_specs=out_specs, out_shape=out_shape,
        scratch_shapes=scratch, compiler_params=_cparams(("parallel", "parallel", "arbitrary")),
        name=f"banded_u{units}_d{n_delta}",
    )(q_arr, *([k_arr] * n_blk), *([vt_arr] * n_blk), bias, ml)
    return res if want_lse else res[0]


def _compress_kernel(x_ref, w1t_ref, w1b_ref, ptop_ref, pbot_ref, w2d_ref, w2t_ref, kc_ref, vct_ref):
    nchunk = x_ref.shape[1]
    c = jnp.concatenate([x_ref[p] for p in range(CMP_STRIDE)], axis=1).astype(_F32)
    top = jnp.dot((c + ptop_ref[...]).astype(_BF), w1t_ref[...], preferred_element_type=_F32)
    bot = jnp.dot((c + pbot_ref[...]).astype(_BF), w1b_ref[...], preferred_element_type=_F32)
    hid = top + pltpu.roll(bot, nchunk - 1, 0)
    act = jax.nn.gelu(hid, approximate=True).astype(_BF)
    kc_ref[...] = jnp.dot(act, w2d_ref[...], preferred_element_type=_F32).astype(kc_ref.dtype)
    vct_ref[...] = lax.dot_general(w2t_ref[...], act, _NT, preferred_element_type=_F32).astype(vct_ref.dtype)


def _compress(x16, col0, w1t, w1b, ptop, pbot, w2d, w2t, nb, nchunk):
    two, nkv = 2, N_KV
    width = CMP_STRIDE * LANES
    sq = pl.Squeezed()
    return pl.pallas_call(
        _compress_kernel,
        grid=(two, nb, nkv),
        in_specs=[pl.BlockSpec((CMP_STRIDE, nchunk, LANES), lambda t, b, h: (0, b, col0 + t * nkv + h)),
                  pl.BlockSpec((sq, width, CMP_HIDDEN), lambda t, b, h: (t, 0, 0)),
                  pl.BlockSpec((sq, width, CMP_HIDDEN), lambda t, b, h: (t, 0, 0)),
                  pl.BlockSpec((sq, 1, width), lambda t, b, h: (t, 0, 0)),
                  pl.BlockSpec((sq, 1, width), lambda t, b, h: (t, 0, 0)),
                  pl.BlockSpec((sq, CMP_HIDDEN, 2 * HD), lambda t, b, h: (t, 0, 0)),
                  pl.BlockSpec((sq, HD, CMP_HIDDEN), lambda t, b, h: (t, 0, 0))],
        out_specs=[pl.BlockSpec((sq, sq, sq, nchunk, 2 * HD), lambda t, b, h: (t, b, h, 0, 0)),
                   pl.BlockSpec((sq, sq, sq, HD, nchunk), lambda t, b, h: (t, b, h, 0, 0))],
        out_shape=[jax.ShapeDtypeStruct((two, nb, nkv, nchunk, 2 * HD), _BF),
                   jax.ShapeDtypeStruct((two, nb, nkv, HD, nchunk), _BF)],
        compiler_params=_cparams(("parallel", "parallel", "parallel")),
        name="nsa_compress",
    )(x16, w1t, w1b, ptop, pbot, w2d, w2t)


def _select_kernel(q_ref, kc_ref, vct_ref, ovt_ref, d0_ref, cb_ref, ocmp_ref, sel_ref, cnt_ref, ot_ref):
    i = pl.program_id(1)
    t0 = (i * TILE).astype(_F32)
    qs = [_stack_queries(q_ref, 0, [hkv * GQA + j for j in range(GQA)]) for hkv in range(N_KV)]
    nsel = ovt_ref.shape[0]
    nchunk = kc_ref.shape[1] // TILE

    def compressed(rows):
        def one(hkv):
            s = (lax.dot_general(kc_ref[hkv, 0:rows, :], qs[hkv], _NT, preferred_element_type=_F32)
                 + cb_ref[hkv, 0:rows, :])
            sb = jnp.where(d0_ref[0:rows, :] + t0 >= 0, s, NEG)
            m = jnp.max(sb, axis=0, keepdims=True)
            e = jnp.exp2(sb - m)
            den = jnp.sum(e, axis=0, keepdims=True)
            p = e * jnp.where(m > 0.5 * NEG, 1.0 / den, 0.0)
            o = jnp.dot(vct_ref[hkv, :, 0:rows], p.astype(_BF), preferred_element_type=_F32)
            psum = p[:, 0:TILE]
            for j in range(1, GQA):
                psum = psum + p[:, j * TILE:(j + 1) * TILE]
            hi = psum.astype(_BF)
            lo = (psum - hi.astype(_F32)).astype(_BF)
            imp = (jnp.dot(ovt_ref[:, 0:rows], hi, preferred_element_type=_F32)
                   + jnp.dot(ovt_ref[:, 0:rows], lo, preferred_element_type=_F32))
            return o, imp
        return lambda: tuple(one(hkv) for hkv in range(N_KV))

    last_block = (i * TILE + TILE - CMP_BLK) // CMP_STRIDE
    res = lax.switch(jnp.minimum(last_block // TILE, nchunk - 1),
                     [compressed((c + 1) * TILE) for c in range(nchunk)])
    for hkv, (o, _) in enumerate(res):
        for j in range(GQA):
            h = hkv * GQA + j
            ot_ref[h * HD:(h + 1) * HD, :] = o[:, j * TILE:(j + 1) * TILE]
    ocmp_ref[...] = ot_ref[...].T.astype(ocmp_ref.dtype)
    jf = lax.broadcasted_iota(jnp.int32, (nsel, TILE), 0).astype(_F32)
    qlane = lax.broadcasted_iota(jnp.int32, (nsel, TILE), 1)
    cur = (i * (TILE // SEL_BLK)).astype(_F32) + jnp.where(qlane >= SEL_BLK, 1.0, 0.0)
    forced = (jf == 0.0) | (jf == cur) | (jf == cur - 1.0)
    free = (jf <= cur) & jnp.logical_not(forced)
    sel0 = jnp.where(forced, 1.0, 0.0)
    init = tuple((jnp.where(free, imp, -2.0), sel0) for _, imp in res)

    def pick(carry):
        work, sel = carry
        mx = jnp.max(work, axis=0, keepdims=True)
        first = jnp.min(jnp.where(work == mx, jf, 1e9), axis=0, keepdims=True)
        hit = (jf == first) & (mx >= 0.0)
        return jnp.where(hit, -2.0, work), jnp.where(hit, 1.0, sel)

    done = lax.fori_loop(0, N_SEL - N_FORCED, lambda _, cs: tuple(pick(c) for c in cs), init)
    ones = jnp.ones((8, TILE), _BF)
    for hkv, (_, sel) in enumerate(done):
        sel_ref[hkv] = sel
        cnt_ref[hkv] = lax.dot_general(ones, sel.astype(_BF), _NT, preferred_element_type=_F32)


def _nsa_select(h1, kc, vct, ovt, d0, cb, nb, seq):
    nq = seq // TILE
    ncmp = kc.shape[2]
    nsel = seq // SEL_BLK
    sq = pl.Squeezed()
    qcb = C_BQ // (4 * LANES)
    return pl.pallas_call(
        _select_kernel,
        grid=(nb, nq),
        in_specs=[pl.BlockSpec((TILE, 4 * LANES), lambda b, i: (b * nq + i, qcb)),
                  pl.BlockSpec((sq, N_KV, ncmp, 2 * HD), lambda b, i: (b, 0, 0, 0)),
                  pl.BlockSpec((sq, N_KV, HD, ncmp), lambda b, i: (b, 0, 0, 0)),
                  pl.BlockSpec((nsel, ncmp), lambda b, i: (0, 0)),
                  pl.BlockSpec((ncmp, GQA * TILE), lambda b, i: (0, 0)),
                  pl.BlockSpec((N_KV, ncmp, GQA * TILE), lambda b, i: (0, 0, 0))],
        out_specs=[pl.BlockSpec((TILE, 4 * LANES), lambda b, i: (b * nq + i, 0)),
                   pl.BlockSpec((sq, N_KV, sq, nsel, TILE), lambda b, i: (b, 0, i, 0, 0)),
                   pl.BlockSpec((sq, N_KV, sq, 8, nsel), lambda b, i: (b, 0, i, 0, 0))],
        out_shape=[jax.ShapeDtypeStruct((nb * seq, 4 * LANES), _BF),
                   jax.ShapeDtypeStruct((nb, N_KV, nq, nsel, TILE), _F32),
                   jax.ShapeDtypeStruct((nb, N_KV, nq, 8, nsel), _F32)],
        scratch_shapes=[pltpu.VMEM((4 * LANES, TILE), _F32)],
        compiler_params=_cparams(("parallel", "arbitrary")),
        name="nsa_select",
    )(h1, kc, vct, ovt, d0, cb)


def _selected_kernel(*refs, nq, nprob):
    flag_ref, q_ref, k_ref = refs[0:3]
    vt_refs = refs[3:3 + nprob]
    sel_ref, qfeat_ref, kfeat_ref, dmask_ref, o_ref, ot_ref, list_ref = refs[3 + nprob:]
    hkv = pl.program_id(0)
    i = pl.program_id(1)
    n = GQA * TILE
    lstride = nq + 2 * SEL_WIDE
    sub = lax.broadcasted_iota(jnp.int32, (8, TILE), 0)

    def problem(pb):
        fbase = ((pb * N_KV + hkv) * nq + i) * nq
        lbase = pb * lstride
        qa = jnp.concatenate([_stack_queries(q_ref.at[pb], 0, list(range(GQA))), qfeat_ref[hkv]], axis=1)

        def scores(jjs):
            rows = []
            for jj in jjs:
                kt = k_ref[pb, pl.ds(pl.multiple_of(jj * TILE, TILE), TILE), :]
                tiles_back = ((jj - i) * TILE).astype(_F32)
                kf = (kfeat_ref[0] + tiles_back * kfeat_ref[1]).astype(_BF)
                rows.append(jnp.concatenate([kt, kf], axis=1))
            s = lax.dot_general(jnp.concatenate(rows, axis=0), qa, _NT, preferred_element_type=_F32)
            return [s[w * TILE:(w + 1) * TILE, :] for w in range(len(jjs))]

        def scan(jj, cnt):
            list_ref[lbase + cnt] = jj
            return cnt + (flag_ref[fbase + jj] > 0).astype(jnp.int32)

        n_act = lax.fori_loop(0, i, scan, jnp.int32(0))
        for w in range(SEL_WIDE):
            list_ref[lbase + n_act + w] = 0

        def sel_row(blk8, r):
            row = jnp.max(jnp.where(sub == r, blk8, 0.0), axis=0, keepdims=True)
            return jnp.concatenate([row] * GQA, axis=1)

        def list_mask(idx):
            jj = list_ref[lbase + idx]
            ok = idx < n_act
            blk8 = sel_ref[pb, pl.ds(pl.multiple_of((jj // 4) * 8, 8), 8), :]
            r = jnp.where(ok, (jj % 4) * 2, -2)
            r0 = jnp.where(sel_row(blk8, r) > 0, 0.0, NEG)
            r1 = jnp.where(sel_row(blk8, r + 1) > 0, 0.0, NEG)
            return jj, jnp.concatenate([jnp.broadcast_to(r0, (SEL_BLK, n)), jnp.broadcast_to(r1, (SEL_BLK, n))], axis=0)

        def masked_scores(idxs, diag=False):
            entries = [list_mask(idx) for idx in idxs]
            if diag:
                entries = [(i, dmask_ref[...])] + entries
            ss = scores([jj for jj, _ in entries])
            return [(s + bias, jj) for s, (jj, bias) in zip(ss, entries)]

        def update(state, tiles):
            m, acc = state
            mn = m
            for s, _ in tiles:
                mn = jnp.maximum(mn, jnp.max(s, axis=0, keepdims=True))
            alpha = jnp.exp2(m - mn)
            vt = jnp.concatenate([vt_refs[pb][:, pl.ds(pl.multiple_of(jj * TILE, TILE), TILE)] for _, jj in tiles], axis=1)
            p = jnp.concatenate([_probs(s, mn) for s, _ in tiles], axis=0)
            pv = jnp.dot(_with_ones(vt), p, preferred_element_type=_F32)
            return mn, alpha * acc + pv

        def first(state):
            return update(state, masked_scores(range(SEL_WIDE - 1), diag=True))

        def group(t, state):
            base = SEL_WIDE - 1 + t * SEL_WIDE
            return update(state, masked_scores([base + w for w in range(SEL_WIDE)]))

        ngroups = (jnp.maximum(n_act - (SEL_WIDE - 1), 0) + SEL_WIDE - 1) // SEL_WIDE
        return first, group, ngroups

    probs = [problem(pb) for pb in range(nprob)]
    init = (jnp.full((1, n), NEG, _F32), jnp.zeros((HD + ONES_ROWS, n), _F32))
    states = tuple(first(init) for first, _, _ in probs)
    ngroups = probs[0][2]
    for _, _, ng in probs[1:]:
        ngroups = jnp.maximum(ngroups, ng)
    states = lax.fori_loop(0, ngroups, lambda t, sts: tuple(group(t, st) for (_, group, _), st in zip(probs, sts)),
                           states)
    for pb, (m, acc) in enumerate(states):
        o = acc[0:HD, :] * (1.0 / acc[HD:HD + 1, :])
        for j in range(GQA):
            ot_ref[j * HD:(j + 1) * HD, :] = o[:, j * TILE:(j + 1) * TILE]
        o_ref[pb] = ot_ref[...].T.astype(o_ref.dtype)


def _nsa_selected(flags, h1, v1t, sel, qfeat, kfeat, dmask, nb, seq):
    nq = seq // TILE
    nsel = seq // SEL_BLK
    sq = pl.Squeezed()
    qcb = C_BQ // (2 * LANES)
    kcb = C_BSK // LANES
    vrb = R_BSV // HD
    h3 = h1.reshape(nb, seq, h1.shape[-1])
    grid_spec = pltpu.PrefetchScalarGridSpec(
        num_scalar_prefetch=1,
        grid=(N_KV, nq),
        in_specs=[pl.BlockSpec((nb, TILE, 2 * LANES), lambda h, i, f: (0, i, qcb + h)),
                  pl.BlockSpec((nb, seq, LANES), lambda h, i, f: (0, 0, kcb + h))]
                 + [pl.BlockSpec((sq, HD, seq), functools.partial(lambda h, i, f, b: (0, vrb + h, b), b=b))
                    for b in range(nb)]
                 + [pl.BlockSpec((nb, sq, sq, nsel, TILE), lambda h, i, f: (0, h, i, 0, 0)),
                    pl.BlockSpec((N_KV, GQA * TILE, LANES), lambda h, i, f: (0, 0, 0)),
                    pl.BlockSpec((2, TILE, LANES), lambda h, i, f: (0, 0, 0)),
                    pl.BlockSpec((TILE, GQA * TILE), lambda h, i, f: (0, 0))],
        out_specs=pl.BlockSpec((nb, TILE, 2 * LANES), lambda h, i, f: (0, i, h)),
        scratch_shapes=[pltpu.VMEM((2 * LANES, TILE), _F32), pltpu.SMEM((nb * (nq + 2 * SEL_WIDE),), jnp.int32)],
    )
    return pl.pallas_call(
        functools.partial(_selected_kernel, nq=nq, nprob=nb),
        grid_spec=grid_spec,
        out_shape=jax.ShapeDtypeStruct((nb, seq, 4 * LANES), _BF),
        compiler_params=_cparams(("parallel", "arbitrary")),
        name="nsa_selected",
    )(flags, h3, h3, *([v1t] * nb), sel, qfeat, kfeat, dmask)


def _merge_kernel(oa0, oa1, oa2, la0, la1, la2, ocmp, osel, owin, oc, ag, bg, cg, gate, mg0, mg1, mg2,
                  x_ref, wb_ref, wo_ref, lng_ref, lnb_ref, ex_ref, xo_ref, xb_ref, *u_refs, alpha):
    f = lambda r: r[...].astype(_F32)
    sig = lambda v: 0.5 * jnp.tanh(0.5 * v) + 0.5
    silu = lambda v: v * sig(v)
    w = N_HEADS * HD
    sg = sig(f(gate))
    hi = sg.astype(_BF)
    lo = (sg - hi.astype(_F32)).astype(_BF)
    gx = (jnp.dot(hi, ex_ref[...], preferred_element_type=_F32)
          + jnp.dot(lo, ex_ref[...], preferred_element_type=_F32))
    def unfold(src, dst):
        r, n = src.shape[0], src.shape[1]
        for c in range(r):
            blk = src[c].astype(_F32)
            for j in range(dst.shape[0]):
                dst[j, pl.ds(c, n, stride=r), :] = blk[:, j * LANES:(j + 1) * LANES]
        return jnp.concatenate([dst[j] for j in range(dst.shape[0])], axis=1)

    o1, l1, o2, l2 = (unfold(src, dst) for src, dst in zip((oa1, la1, oa2, la2), u_refs))
    l0 = la0[...]
    mx = jnp.maximum(jnp.maximum(l0, l1), l2)
    e0, e1, e2 = jnp.exp(l0 - mx), jnp.exp(l1 - mx), jnp.exp(l2 - mx)
    ya = (e0 * f(oa0) + e1 * o1 + e2 * o2) * (1.0 / (e0 + e1 + e2))
    ya = ya * silu(ag[...])
    yb = gx[:, 0:w] * f(ocmp) + gx[:, w:2 * w] * f(osel) + gx[:, 2 * w:3 * w] * f(owin)
    yb = yb * silu(bg[...])
    yc = f(oc) * silu(cg[...])
    merged = (sig(mg0[...]) * jnp.dot(ya.astype(_BF), wb_ref[0], preferred_element_type=_F32)
              + sig(mg1[...]) * jnp.dot(yb.astype(_BF), wb_ref[1], preferred_element_type=_F32)
              + sig(mg2[...]) * jnp.dot(yc.astype(_BF), wb_ref[2], preferred_element_type=_F32))
    y = jnp.dot(merged.astype(_BF), wo_ref[...], preferred_element_type=_F32)
    z = alpha * x_ref[...] + y
    mu = jnp.mean(z, axis=-1, keepdims=True)
    zc = z - mu
    var = jnp.mean(zc * zc, axis=-1, keepdims=True)
    out = zc * lax.rsqrt(var + LN_EPS) * lng_ref[...] + lnb_ref[...]
    xo_ref[...] = out
    xb_ref[...] = out.astype(_BF)


def _merge(oa, la, ocmp, osel, owin, oc, h1, x, wb, wo, lng, lnb, expand, alpha, tm=256):
    ntok, dm = x.shape
    w = 4 * LANES
    tok = lambda c: pl.BlockSpec((tm, w), lambda i: (i, c))
    h1s = lambda col, width: pl.BlockSpec((tm, width), lambda i: (i, col // width))
    cls = lambda a: pl.BlockSpec((a.shape[0], tm // a.shape[0], w), lambda i: (0, i, 0))
    in_specs = ([tok(0), cls(oa[1]), cls(oa[2]), tok(0), cls(la[1]), cls(la[2])] + [tok(0)] * 4
                + [h1s(C_AG, w), h1s(C_BG, w), h1s(C_CG, w)]
                + [h1s(C_BGATE, LANES)]
                + [h1s(C_MG + j * dm, dm) for j in range(3)]
                + [pl.BlockSpec((tm, dm), lambda i: (i, 0)),
                   pl.BlockSpec(wb.shape, lambda i: (0, 0, 0)),
                   pl.BlockSpec(wo.shape, lambda i: (0, 0)),
                   pl.BlockSpec((1, dm), lambda i: (0, 0)),
                   pl.BlockSpec((1, dm), lambda i: (0, 0)),
                   pl.BlockSpec(expand.shape, lambda i: (0, 0))])
    return pl.pallas_call(
        functools.partial(_merge_kernel, alpha=alpha),
        grid=(ntok // tm,),
        in_specs=in_specs,
        out_specs=[pl.BlockSpec((tm, dm), lambda i: (i, 0))] * 2,
        out_shape=[jax.ShapeDtypeStruct((ntok, dm), _F32), jax.ShapeDtypeStruct((ntok, dm), _BF)],
        scratch_shapes=[pltpu.VMEM((w // LANES, tm, LANES), _F32)] * 4,
        compiler_params=_cparams(("parallel",)),
        name="merge_norm",
    )(*oa, *la, ocmp, osel, owin, oc, *([h1] * 7), x, wb, wo, lng, lnb, expand)


def _alibi(n):
    return 2.0 ** (-8.0 * np.arange(1, n + 1, dtype=np.float64) / n)


def _prep_weights(w_in, b_in, w_cmp1, w_cmp2, cmp_pos, w_branch, w_out):
    widths = (1536, 1536, 1536, 512, 512, 128, 128, 128, 128, 128, 128, 512, 24, 512, 128, 128, 512, 3072)
    names = ("aq", "ak", "av", "ag", "bq", "bck", "bcv", "bsk", "bsv", "bwk", "bwv", "bg", "bgate", "cq", "ck", "cv", "cg", "mg")
    off = dict(zip(names, np.cumsum((0,) + widths[:-1]).tolist()))

    def cols(sl, scale=1.0):
        w, b = w_in[:, :, sl], b_in[:, sl]
        return (w, b) if scale == 1.0 else (w * scale, b * scale)

    def rng(name, start, n):
        return slice(off[name] + start, off[name] + start + n)

    def cat(parts):
        return jnp.concatenate([p[0] for p in parts], axis=2), jnp.concatenate([p[1] for p in parts], axis=1)

    def dup(name):
        return cat([cols(rng(name, kv * HD, HD)) for kv in range(N_KV) for _ in range(2)])

    ngate = 3 * N_HEADS
    pad = H1_COLS - C_BGATE - ngate
    zeros = (jnp.zeros(w_in.shape[:2] + (pad,), w_in.dtype), jnp.zeros(b_in.shape[:1] + (pad,), b_in.dtype))
    tok = [cols(rng("mg", 0, 3072)), cols(rng("aq", 0, 512), Q_SCALE), cols(rng("ak", 0, 512)),
           cols(rng("bq", 0, 512), Q_SCALE), cols(rng("cq", 0, 512), Q_SCALE),
           cols(rng("ag", 0, 512)), cols(rng("bg", 0, 512)), cols(rng("cg", 0, 512)),
           dup("bsk"), dup("bwk"), dup("ck"),
           cols(rng("bgate", 0, ngate)), zeros]
    w1 = jnp.concatenate([t[0].astype(_BF) for t in tok], axis=2)
    b1 = jnp.concatenate([t[1] for t in tok], axis=1)[:, None, :]
    wqk, bqk, wvt, bvt = [], [], [], []
    for gi in (1, 2):
        wq, bq = cols(rng("aq", gi * 512, 512), Q_SCALE)
        wk, bk = cols(rng("ak", gi * 512, 512))
        extra = []
        if A_PATTERNS[gi][1] == CMP_STRIDE:
            zw, zb = jnp.zeros(w_in.shape[:2] + (HD,), w_in.dtype), jnp.zeros(b_in.shape[:1] + (HD,), b_in.dtype)
            for name in ("bck", "bcv"):
                for kv in range(N_KV):
                    extra += [cols(rng(name, kv * HD, HD)), (zw, zb)]
        wqk.append(jnp.concatenate([wq, wk] + [e[0] for e in extra], axis=2).astype(_BF))
        bqk.append(jnp.concatenate([bq, bk] + [e[1] for e in extra], axis=1)[:, None, :])
        wv, bv = cols(rng("av", gi * 512, 512))
        wvt.append(wv.transpose(0, 2, 1).astype(_BF))
        bvt.append(bv[:, :, None])
    feat = [cols(rng("av", 0, 512)), cols(rng("bsv", 0, 128)), cols(rng("bwv", 0, 128)), cols(rng("cv", 0, 128))]
    wv1t = jnp.concatenate([t[0] for t in feat], axis=2).transpose(0, 2, 1).astype(_BF)
    bv1t = jnp.concatenate([t[1] for t in feat], axis=1)[:, :, None]
    depth = w_cmp1.shape[0]
    w1p = jnp.pad(w_cmp1.reshape(depth, 2, CMP_BLK, HD, CMP_HIDDEN), ((0, 0),) * 3 + ((0, LANES - HD), (0, 0)))
    w1p = w1p.reshape(depth, 2, 2, CMP_STRIDE * LANES, CMP_HIDDEN).astype(_BF)
    posp = jnp.pad(cmp_pos, ((0, 0),) * 3 + ((0, LANES - HD),)).reshape(depth, 2, 2, 1, CMP_STRIDE * LANES)
    cmpw = dict(w1t=w1p[:, :, 0], w1b=w1p[:, :, 1], ptop=posp[:, :, 0], pbot=posp[:, :, 1],
                w2d=jnp.concatenate([w_cmp2, w_cmp2], axis=-1).astype(_BF),
                w2t=w_cmp2.transpose(0, 1, 3, 2).astype(_BF))
    return dict(w1=w1, b1=b1, wqk=wqk, bqk=bqk, wvt=wvt, bvt=bvt, wv1t=wv1t, bv1t=bv1t, cmp=cmpw,
                wb=w_branch.astype(_BF), wo=w_out.astype(_BF))


def _constants(seq, sinks):
    a_slopes = _alibi(3 * N_HEADS).reshape(3, N_HEADS)
    bc_slopes = _alibi(N_HEADS)
    no_sink = lambda units, n: jnp.stack([jnp.full((units, n), NEG, _F32), jnp.zeros((units, n), _F32)], axis=1)
    c = {}
    c["a_bias"] = [_banded_bias(a_slopes[gi], win // dil, dil, 2, 2) for gi, (win, dil) in enumerate(A_PATTERNS)]
    c["a_ml"] = no_sink(N_HEADS // 2, 2 * TILE)
    c["bw_bias"] = _banded_bias(bc_slopes, NSA_WINDOW - 1, 1, 1 + NSA_WINDOW // TILE, GQA)
    c["bw_ml"] = no_sink(N_KV, GQA * TILE)
    c["c_bias"] = _banded_bias(bc_slopes, C_WINDOW - 1, 1, 2, GQA)
    sink_rows = jnp.repeat(sinks.astype(_F32).reshape(-1, N_KV, GQA) * LOG2E, TILE, axis=-1)
    c["c_ml"] = jnp.stack([sink_rows, jnp.ones_like(sink_rows)], axis=2)
    lane_slopes = np.repeat(bc_slopes.reshape(N_KV, GQA), TILE, axis=-1)
    ncmp = seq // CMP_STRIDE
    nsel = seq // SEL_BLK
    cs = np.arange(ncmp) * CMP_STRIDE
    ss = np.arange(nsel) * SEL_BLK
    overlap = (cs[None, :] < ss[:, None] + SEL_BLK) & (cs[None, :] + CMP_BLK > ss[:, None]) & (cs[None, :] + CMP_BLK <= seq)
    c["ovt"] = jnp.asarray(overlap, _BF)
    qq = np.tile(np.arange(TILE), GQA)[None, :]
    d0 = qq - (cs[:, None] + CMP_BLK - 1)
    c["d0"] = jnp.asarray(d0, _F32)
    c["cb"] = jnp.asarray(-(lane_slopes * LOG2E)[:, None, :] * d0[None], _F32)
    kk = np.arange(TILE)[:, None]
    c["dmask"] = jnp.asarray(np.where(kk <= qq, 0.0, NEG), _F32)
    rest = lane_slopes * LOG2E
    pieces = []
    for _ in range(3):
        piece = rest.astype(ml_dtypes.bfloat16).astype(np.float64)
        pieces.append(piece)
        rest = rest - piece
    qfeat = np.zeros((N_KV, GQA * TILE, LANES))
    kfeat = np.zeros((2, TILE, LANES))
    for a, piece in enumerate(pieces):
        qfeat[:, :, a] = piece
        qfeat[:, :, 3 + a] = piece
        kfeat[0, :, a] = np.arange(TILE)
        kfeat[1, :, 3 + a] = 1.0
    c["qfeat"] = jnp.asarray(qfeat, _BF)
    c["kfeat"] = jnp.asarray(kfeat, _F32)
    expand = np.zeros((LANES, 3 * N_HEADS * HD))
    for h in range(N_HEADS):
        for j in range(3):
            expand[h * 3 + j, j * N_HEADS * HD + h * HD:j * N_HEADS * HD + (h + 1) * HD] = 1.0
    c["expand"] = jnp.asarray(expand, _BF)
    return c


def kernel(x, w_in, b_in, w_cmp1, w_cmp2, cmp_pos, sinks, w_branch, w_out, ln_g, ln_b):
    nb, seq, dm = x.shape
    depth = w_in.shape[0]
    ntok = nb * seq
    alpha = (2 * depth) ** 0.25
    wts = _prep_weights(w_in, b_in, w_cmp1, w_cmp2, cmp_pos, w_branch, w_out)
    cst = _constants(seq, sinks)
    nchunk = seq // CMP_STRIDE
    nq = seq // TILE
    xf = x.reshape(ntok, dm)
    xb = xf.astype(_BF)
    for l in range(depth):
        h1 = _matmul(xb, wts["w1"][l], wts["b1"][l], tm=H1_TM, tn=H1_TN)
        v1t = _matmul_t(xb, wts["wv1t"][l], wts["bv1t"][l])[None]
        h3 = h1[None]
        oa, la = [], []
        for gi, (win, dil) in enumerate(A_PATTERNS):
            if dil == 1:
                qk, qcb, kcb, vt = h3, C_AQ0 // 512, C_AK0 // 512, v1t
            else:
                qk = _matmul_fold(xf, wts["wqk"][gi - 1][l], wts["bqk"][gi - 1][l], dil)
                vt = _matmul_t_fold(xf, wts["wvt"][gi - 1][l], wts["bvt"][gi - 1][l], dil)
                qcb, kcb = 0, 1
                if dil == CMP_STRIDE:
                    x16 = qk
            o, lse = _banded_attention(qk, qcb, qk, kcb, 4 * LANES, vt, R_AV0 // 512, 4 * LANES, cst["a_bias"][gi],
                                       cst["a_ml"], nb, seq // dil, True, nclass=dil)
            oa.append(o[0] if dil == 1 else o)
            la.append(lse[0] if dil == 1 else lse)
        cw = wts["cmp"]
        kc, vct = _compress(x16, C16_CMP // LANES, cw["w1t"][l], cw["w1b"][l], cw["ptop"][l], cw["pbot"][l], cw["w2d"][l],
                            cw["w2t"][l], nb, nchunk)
        ocmp, sel, cnt = _nsa_select(h1, kc[0], vct[1], cst["ovt"], cst["d0"], cst["cb"], nb, seq)
        flags = (cnt[:, :, :, 0, :].reshape(nb, N_KV, nq, nq, 2).sum(-1) > 0).astype(jnp.int32).reshape(-1)
        osel = _nsa_selected(flags, h1, v1t, sel, cst["qfeat"], cst["kfeat"], cst["dmask"], nb, seq).reshape(ntok, 4 * LANES)
        owin = _banded_attention(h3, C_BQ // 512, h3, C_BWK // (2 * LANES), 2 * LANES, v1t, R_BWV // LANES, LANES,
                                 cst["bw_bias"], cst["bw_ml"], nb, seq, False)[0]
        oc = _banded_attention(h3, C_CQ // 512, h3, C_CK // (2 * LANES), 2 * LANES, v1t, R_CV // LANES, LANES,
                               cst["c_bias"], cst["c_ml"][l], nb, seq, False)[0]
        xf, xb = _merge(oa, la, ocmp, osel, owin, oc, h1, xf, wts["wb"][l], wts["wo"][l],
                        ln_g[l][None, :], ln_b[l][None, :], cst["expand"], alpha)
    return xf.reshape(nb, seq, dm).astype(x.dtype)
```
